```python
import math
import jax, jax.numpy as jnp
from jax import lax
import numpy as np

D_MODEL = 1024
BATCH = 2
SEQ = 8192
DEPTH = 2
DEC_BATCH = 128
DEC_SEQ = 4
PAST_LEN = 16384
PAGE_SIZE = 128

RW_HEADS = 8
RW_HEAD = 64
RW_DIM = RW_HEADS * RW_HEAD
LORA_W = 64
LORA_A = 64
LORA_G = 128
RW_COLS = 3 * RW_DIM + LORA_W + LORA_A + LORA_G
MLA_HEADS = 8
QK_NOPE = 64
QK_ROPE = 32
V_HEAD = 64
Q_LORA = 384
KV_LORA = 128
MLA_COLS = Q_LORA + KV_LORA + QK_ROPE
N_IN = RW_COLS + MLA_COLS + 2 * D_MODEL
D_FF = -(-(8 * D_MODEL) // (3 * 256)) * 256

ROPE_THETA = 10000.0
NORM_EPS = 1e-6
GN_EPS = 64e-5
DECAY_SCALE = math.exp(-0.5)
ATTN_SCALE = (QK_NOPE + QK_ROPE) ** -0.5
Q_BLOCK = 128

kernel_name = 'hybrid_rwkv7_mla_step'


def rms_norm(x, g):
    xf = x.astype(jnp.float32)
    y = xf * lax.rsqrt(jnp.mean(xf * xf, axis=-1, keepdims=True) + NORM_EPS)
    return (y * g.astype(jnp.float32)).astype(x.dtype)


def rope(x, pos):
    half = x.shape[-1] // 2
    inv = ROPE_THETA ** (-jnp.arange(half, dtype=jnp.float32) / half)
    ang = pos.astype(jnp.float32)[:, None] * inv[None, :]
    cos = jnp.cos(ang)[:, None, :]
    sin = jnp.sin(ang)[:, None, :]
    xf = x.astype(jnp.float32)
    x1, x2 = xf[..., :half], xf[..., half:]
    return jnp.concatenate([x1 * cos - x2 * sin, x2 * cos + x1 * sin], axis=-1).astype(x.dtype)


def rwkv7_branch(p_rw, shift0, wkv0, mu, w_up, w0, a_up, a0, g_up, k_k, k_a, r_k, gn_w, gn_b, w_o):
    f32 = jnp.float32
    b, t, _ = p_rw.shape
    prev = jnp.concatenate([shift0[:, None, :].astype(p_rw.dtype), p_rw[:, :-1]], axis=1)
    m = p_rw + (prev - p_rw) * mu
    cuts = [RW_DIM, 2 * RW_DIM, 3 * RW_DIM, 3 * RW_DIM + LORA_W, 3 * RW_DIM + LORA_W + LORA_A]
    r, k, v, xw, xa, xg = jnp.split(m, cuts, axis=-1)
    decay = jnp.exp(-DECAY_SCALE * jax.nn.sigmoid((w0 + jnp.tanh(xw) @ w_up).astype(f32)))
    a = jax.nn.sigmoid((a0 + xa @ a_up).astype(f32))
    g = (jax.nn.sigmoid(xg) @ g_up).astype(f32)
    heads = lambda z: z.astype(f32).reshape(b, t, RW_HEADS, RW_HEAD)
    kk = heads(k * k_k)
    kk = kk / jnp.maximum(jnp.sqrt(jnp.sum(kk * kk, axis=-1, keepdims=True)), 1e-12)
    k_mod = k.astype(f32) * (1.0 + (a - 1.0) * k_a.astype(f32))
    r4, k4, v4, w4, a4 = heads(r), heads(k_mod), heads(v), heads(decay), heads(a)

    def step(S, inp):
        r_t, w_t, k_t, v_t, kk_t, a_t = inp
        sa = -jnp.einsum('bhij,bhj->bhi', S, kk_t)
        S = (S * w_t[:, :, None, :] + sa[..., None] * (kk_t * a_t)[:, :, None, :]
             + v_t[..., None] * k_t[:, :, None, :])
        return S, jnp.einsum('bhij,bhj->bhi', S, r_t)

    xs = tuple(jnp.swapaxes(z, 0, 1) for z in (r4, w4, k4, v4, kk, a4))
    S, ys = lax.scan(step, wkv0.astype(f32), xs)
    y = jnp.swapaxes(ys, 0, 1)
    mean = jnp.mean(y, axis=-1, keepdims=True)
    var = jnp.mean(jnp.square(y - mean), axis=-1, keepdims=True)
    y = ((y - mean) * lax.rsqrt(var + GN_EPS)).reshape(b, t, RW_DIM) * gn_w.astype(f32) + gn_b.astype(f32)
    bonus = jnp.sum(r4 * k4 * r_k.astype(f32), axis=-1, keepdims=True) * v4
    y = (y + bonus.reshape(b, t, RW_DIM)) * g
    return y.astype(p_rw.dtype) @ w_o, S.astype(wkv0.dtype), p_rw[:, -1]


def mla_attend_causal(q_lat, q_rope, ckv, kr):
    b, t = q_lat.shape[:2]
    nb = t // Q_BLOCK
    blocks = lambda z: jnp.swapaxes(z.reshape((b, nb, Q_BLOCK) + z.shape[2:]), 0, 1)
    k_pos = jnp.arange(t)

    def one_block(args):
        i, ql, qr = args
        s = (jnp.einsum('bqhc,bsc->bhqs', ql, ckv)
             + jnp.einsum('bqhr,bsr->bhqs', qr, kr)).astype(jnp.float32) * ATTN_SCALE
        q_pos = i * Q_BLOCK + jnp.arange(Q_BLOCK)
        s = jnp.where(k_pos[None, :] <= q_pos[:, None], s, -jnp.inf)
        p = jax.nn.softmax(s, axis=-1).astype(ckv.dtype)
        return jnp.einsum('bhqs,bsc->bqhc', p, ckv)

    o = lax.map(one_block, (jnp.arange(nb), blocks(q_lat), blocks(q_rope)))
    return jnp.swapaxes(o, 0, 1).reshape(b, t, MLA_HEADS, KV_LORA)


def mla_attend_paged(layer, q_lat, q_rope, ckv_new, kr_new, cache_ckv, cache_kr, page_table):
    f32 = jnp.float32
    b, t = q_lat.shape[:2]
    ql = q_lat.astype(f32)
    qr = q_rope.astype(f32)

    def scores(kc, kp):
        return (jnp.einsum('bqhc,bsc->bhqs', ql, kc.astype(f32))
                + jnp.einsum('bqhr,bsr->bhqs', qr, kp.astype(f32))) * ATTN_SCALE

    def update(carry, s, kc):
        m, l_sum, acc = carry
        m_new = jnp.maximum(m, jnp.max(s, axis=-1))
        corr = jnp.exp(m - m_new)
        pe = jnp.exp(s - m_new[..., None])
        return (m_new, l_sum * corr + jnp.sum(pe, axis=-1),
                acc * corr[..., None] + jnp.einsum('bhqs,bsc->bhqc', pe, kc.astype(f32)))

    def page_step(carry, pages):
        kc = cache_ckv[layer, pages]
        kp = cache_kr[layer, pages]
        return update(carry, scores(kc, kp), kc), None

    init = (jnp.full((b, MLA_HEADS, t), -jnp.inf, f32),
            jnp.zeros((b, MLA_HEADS, t), f32),
            jnp.zeros((b, MLA_HEADS, t, KV_LORA), f32))
    carry, _ = lax.scan(page_step, init, page_table.T)
    causal = jnp.arange(t)[None, :] <= jnp.arange(t)[:, None]
    s_self = jnp.where(causal, scores(ckv_new, kr_new), -jnp.inf)
    _, l_sum, acc = update(carry, s_self, ckv_new)
    o = acc / l_sum[..., None]
    return jnp.swapaxes(o, 1, 2).astype(q_lat.dtype)


def hybrid_layer(layer, x, c, pos, attend, shift0, wkv0, P):
    b, t, _ = x.shape
    mod = c @ P['mod_w'][layer] + P['mod_b'][layer]
    sh1, sc1, gt1, sh2, sc2, gt2 = [z[:, None, :] for z in jnp.split(mod, 6, axis=-1)]
    g = P['norm_g'][layer]
    h = rms_norm(x, g[0]) * (1 + sc1) + sh1
    proj = h @ P['w_in'][layer]
    p_rw, p_mla, p_gate = jnp.split(proj, [RW_COLS, RW_COLS + MLA_COLS], axis=-1)
    o_rw, wkv_new, shift_new = rwkv7_branch(
        p_rw, shift0, wkv0, P['rw_mu'][layer], P['rw_w_up'][layer], P['rw_w0'][layer],
        P['rw_a_up'][layer], P['rw_a0'][layer], P['rw_g_up'][layer], P['rw_k_k'][layer],
        P['rw_k_a'][layer], P['rw_r_k'][layer], P['rw_gn_w'][layer], P['rw_gn_b'][layer],
        P['rw_w_o'][layer])
    cq, ckv, kr = jnp.split(p_mla, [Q_LORA, Q_LORA + KV_LORA], axis=-1)
    q = (rms_norm(cq, P['mla_q_g'][layer]) @ P['mla_w_uq'][layer]).reshape(b, t, MLA_HEADS, QK_NOPE + QK_ROPE)
    q_nope, q_rope = q[..., :QK_NOPE], rope(q[..., QK_NOPE:], pos)
    ckv = rms_norm(ckv, P['mla_kv_g'][layer])
    kr = rope(kr[:, :, None, :], pos)[:, :, 0, :]
    q_lat = jnp.einsum('bthn,chn->bthc', q_nope, P['mla_w_uk'][layer])
    o_lat = attend(layer, q_lat, q_rope, ckv, kr)
    o_mla = jnp.einsum('bthc,chv->bthv', o_lat, P['mla_w_uv'][layer]).reshape(b, t, MLA_HEADS * V_HEAD) @ P['mla_w_o'][layer]
    g_rw, g_mla = jnp.split(jax.nn.sigmoid(p_gate), 2, axis=-1)
    mix = (g_rw * o_rw + g_mla * o_mla) @ P['w_out'][layer]
    x = x + gt1 * rms_norm(mix, g[1])
    h2 = rms_norm(x, g[2]) * (1 + sc2) + sh2
    u, v = jnp.split(h2 @ P['ffn_w_in'][layer], 2, axis=-1)
    x = x + gt2 * rms_norm((jax.nn.silu(u) * v) @ P['ffn_w_out'][layer], g[3])
    return x, ckv, kr, wkv_new, shift_new


def run_trunk(x, c, pos, attend, shift0s, wkv0s, P):
    ckvs, krs, wkvs, shifts = [], [], [], []
    for layer in range(DEPTH):
        x, ckv, kr, wkv, sh = hybrid_layer(layer, x, c, pos, attend, shift0s[layer], wkv0s[layer], P)
        ckvs.append(ckv)
        krs.append(kr)
        wkvs.append(wkv)
        shifts.append(sh)
    return x, jnp.stack(ckvs), jnp.stack(krs), jnp.stack(wkvs), jnp.stack(shifts)


def setup_inputs(seed: int = 0) -> dict:
    key = jax.random.key(seed)
    keys = jax.random.split(key, 48)
    idx = [0]

    def nxt():
        idx[0] += 1
        return keys[idx[0] - 1]

    f32 = jnp.float32
    nrm = lambda shape, scale: scale * jax.random.normal(nxt(), shape, f32)
    n_pages = PAST_LEN // PAGE_SIZE
    n_used = DEC_BATCH * n_pages
    n_phys = n_used + max(1, n_used // 4)
    page_table = jax.random.permutation(nxt(), n_phys)[:n_used].reshape(DEC_BATCH, n_pages).astype(jnp.int32)
    return {
        'x_prompt': nrm((BATCH, SEQ, D_MODEL), 1.0),
        'x_sample': nrm((DEC_BATCH, DEC_SEQ, D_MODEL), 1.0),
        'cache_ckv': nrm((DEPTH, n_phys, PAGE_SIZE, KV_LORA), 1.0),
        'cache_kr': nrm((DEPTH, n_phys, PAGE_SIZE, QK_ROPE), 1.0),
        'state_wkv': nrm((DEPTH, DEC_BATCH, RW_HEADS, RW_HEAD, RW_HEAD), 0.1),
        'state_shift': nrm((DEPTH, DEC_BATCH, RW_COLS), 1.0),
        'page_table': page_table,
        'c_prompt': nrm((BATCH, D_MODEL), 1.0),
        'c_sample': nrm((DEC_BATCH, D_MODEL), 1.0),
        'mod_w': nrm((DEPTH, D_MODEL, 6 * D_MODEL), 0.5 * D_MODEL ** -0.5),
        'mod_b': nrm((DEPTH, 6 * D_MODEL), 0.01),
        'norm_g': 1.0 + nrm((DEPTH, 4, D_MODEL), 0.05),
        'w_in': nrm((DEPTH, D_MODEL, N_IN), D_MODEL ** -0.5),
        'rw_mu': jax.random.uniform(nxt(), (DEPTH, RW_COLS), f32),
        'rw_w_up': nrm((DEPTH, LORA_W, RW_DIM), LORA_W ** -0.5),
        'rw_w0': nrm((DEPTH, RW_DIM), 0.5),
        'rw_a_up': nrm((DEPTH, LORA_A, RW_DIM), LORA_A ** -0.5),
        'rw_a0': nrm((DEPTH, RW_DIM), 0.1),
        'rw_g_up': nrm((DEPTH, LORA_G, RW_DIM), LORA_G ** -0.5),
        'rw_k_k': 0.85 + nrm((DEPTH, RW_DIM), 0.05),
        'rw_k_a': 1.0 + nrm((DEPTH, RW_DIM), 0.05),
        'rw_r_k': nrm((DEPTH, RW_HEADS, RW_HEAD), 0.1),
        'rw_gn_w': 1.0 + nrm((DEPTH, RW_DIM), 0.05),
        'rw_gn_b': nrm((DEPTH, RW_DIM), 0.01),
        'rw_w_o': nrm((DEPTH, RW_DIM, D_MODEL), RW_DIM ** -0.5),
        'mla_q_g': 1.0 + nrm((DEPTH, Q_LORA), 0.05),
        'mla_w_uq': nrm((DEPTH, Q_LORA, MLA_HEADS * (QK_NOPE + QK_ROPE)), Q_LORA ** -0.5),
        'mla_kv_g': 1.0 + nrm((DEPTH, KV_LORA), 0.05),
        'mla_w_uk': nrm((DEPTH, KV_LORA, MLA_HEADS, QK_NOPE), KV_LORA ** -0.5),
        'mla_w_uv': nrm((DEPTH, KV_LORA, MLA_HEADS, V_HEAD), KV_LORA ** -0.5),
        'mla_w_o': nrm((DEPTH, MLA_HEADS * V_HEAD, D_MODEL), (MLA_HEADS * V_HEAD) ** -0.5),
        'w_out': nrm((DEPTH, D_MODEL, D_MODEL), D_MODEL ** -0.5),
        'ffn_w_in': nrm((DEPTH, D_MODEL, 2 * D_FF), D_MODEL ** -0.5),
        'ffn_w_out': nrm((DEPTH, D_FF, D_MODEL), D_FF ** -0.5),
    }


def reference(x_prompt, x_sample, cache_ckv, cache_kr, state_wkv, state_shift, page_table,
              c_prompt, c_sample, mod_w, mod_b, norm_g, w_in, rw_mu, rw_w_up, rw_w0, rw_a_up,
              rw_a0, rw_g_up, rw_k_k, rw_k_a, rw_r_k, rw_gn_w, rw_gn_b, rw_w_o, mla_q_g, mla_w_uq,
              mla_kv_g, mla_w_uk, mla_w_uv, mla_w_o, w_out, ffn_w_in, ffn_w_out):
    P = dict(mod_w=mod_w, mod_b=mod_b, norm_g=norm_g, w_in=w_in, rw_mu=rw_mu, rw_w_up=rw_w_up,
             rw_w0=rw_w0, rw_a_up=rw_a_up, rw_a0=rw_a0, rw_g_up=rw_g_up, rw_k_k=rw_k_k,
             rw_k_a=rw_k_a, rw_r_k=rw_r_k, rw_gn_w=rw_gn_w, rw_gn_b=rw_gn_b, rw_w_o=rw_w_o,
             mla_q_g=mla_q_g, mla_w_uq=mla_w_uq, mla_kv_g=mla_kv_g, mla_w_uk=mla_w_uk,
             mla_w_uv=mla_w_uv, mla_w_o=mla_w_o, w_out=w_out, ffn_w_in=ffn_w_in,
             ffn_w_out=ffn_w_out)
    b, t = x_prompt.shape[:2]
    zero_shift = jnp.zeros((DEPTH, b, RW_COLS), x_prompt.dtype)
    zero_wkv = jnp.zeros((DEPTH, b, RW_HEADS, RW_HEAD, RW_HEAD), state_wkv.dtype)
    attend_prompt = lambda layer, ql, qr, ck, kp: mla_attend_causal(ql, qr, ck, kp)
    y_prompt, ckv_p, kr_p, wkv_p, shift_p = run_trunk(
        x_prompt, c_prompt, jnp.arange(t), attend_prompt, zero_shift, zero_wkv, P)
    past = page_table.shape[1] * PAGE_SIZE
    attend_sample = lambda layer, ql, qr, ck, kp: mla_attend_paged(
        layer, ql, qr, ck, kp, cache_ckv, cache_kr, page_table)
    y_sample, ckv_s, kr_s, wkv_s, shift_s = run_trunk(
        x_sample, c_sample, past + jnp.arange(x_sample.shape[1]), attend_sample,
        state_shift, state_wkv, P)
    return (y_prompt, y_sample, ckv_p, kr_p, wkv_p, shift_p, ckv_s, kr_s, wkv_s, shift_s)
```

```python
import functools
import math

import jax
import jax.numpy as jnp
import numpy as np
from jax import lax
from jax.experimental import pallas as pl
from jax.experimental.pallas import tpu as pltpu

F32 = jnp.float32
BF16 = jnp.bfloat16

D_MODEL = 1024
RW_HEADS = 8
RW_HEAD = 64
RW_DIM = RW_HEADS * RW_HEAD
LORA_W = 64
LORA_A = 64
LORA_G = 128
RW_COLS = 3 * RW_DIM + LORA_W + LORA_A + LORA_G
MLA_HEADS = 8
QK_NOPE = 64
QK_ROPE = 32
V_HEAD = 64
Q_LORA = 384
KV_LORA = 128
MLA_COLS = Q_LORA + KV_LORA + QK_ROPE
MLA_PAD = 640
QK_DIM = KV_LORA + QK_ROPE
PAGE_SIZE = 128

ROPE_THETA = 10000.0
NORM_EPS = 1e-6
GN_EPS = 64e-5
DECAY_SCALE = math.exp(-0.5)
ATTN_SCALE = (QK_NOPE + QK_ROPE) ** -0.5

LANES = 128
N_PAIRS = RW_HEADS // 2
VMEM_LIMIT = 56 * 1024 * 1024


def _cparams(sem):
    return pltpu.CompilerParams(dimension_semantics=sem, vmem_limit_bytes=VMEM_LIMIT)


def _dg(a, b, dims):
    return lax.dot_general(a, b, (dims, ((), ())), preferred_element_type=F32)


_NN = ((1,), (0,))
_NT = ((1,), (1,))


def _dot1(a, b, dims=_NN):
    return _dg(a.astype(BF16), b.astype(BF16), dims)


def _split2(a):
    hi = a.astype(BF16)
    lo = (a - hi.astype(F32)).astype(BF16)
    return hi, lo


def _split3(a):
    hi = a.astype(BF16)
    r = a - hi.astype(F32)
    mid = r.astype(BF16)
    lo = (r - mid.astype(F32)).astype(BF16)
    return hi, mid, lo


def _dot3(a, b, dims=_NN):
    a1, a2 = _split2(a)
    b1, b2 = _split2(b)
    return _dg(a1, b1, dims) + (_dg(a1, b2, dims) + _dg(a2, b1, dims))


def _dot_exact_rhs(a, b_bf16):
    a1, a2, a3 = _split3(a)
    return _dg(a1, b_bf16, _NN) + (_dg(a2, b_bf16, _NN) + _dg(a3, b_bf16, _NN))


def _dot_exact_lhs(a_bf16, b):
    b1, b2, b3 = _split3(b)
    return _dg(a_bf16, b1, _NN) + (_dg(a_bf16, b2, _NN) + _dg(a_bf16, b3, _NN))


def _rms(x, g):
    return x * lax.rsqrt(jnp.mean(x * x, axis=-1, keepdims=True) + NORM_EPS) * g


def _sigmoid(x):
    return 1.0 / (1.0 + jnp.exp(-x))


def _mod_kernel(c_ref, w_ref, b_ref, o_ref):
    o_ref[...] = _dot3(c_ref[...], w_ref[...]) + b_ref[...]


def _mod_matmul(c, w, b):
    m, k = c.shape
    n = w.shape[1]
    tn = 1024
    return pl.pallas_call(
        _mod_kernel,
        out_shape=jax.ShapeDtypeStruct((m, n), F32),
        grid=(n // tn,),
        in_specs=[pl.BlockSpec((m, k), lambda j: (0, 0)),
                  pl.BlockSpec((k, tn), lambda j: (0, j)),
                  pl.BlockSpec((1, tn), lambda j: (0, j))],
        out_specs=pl.BlockSpec((m, tn), lambda j: (0, j)),
        compiler_params=_cparams(("arbitrary",)),
        name="mod_matmul",
    )(c, w, b.reshape(1, n))


def _mod_spec(mod, tm, tiles_per_group):
    g, r, d = mod.shape
    if r == 1:
        return pl.BlockSpec((1, 1, d), lambda i: (i // tiles_per_group, 0, 0))
    return pl.BlockSpec((1, r, d), lambda i: (i, 0, 0))


def _full(a):
    nd = a.ndim
    return pl.BlockSpec(a.shape, lambda i: (0,) * nd)


def _rows(tm, width):
    return pl.BlockSpec((tm, width), lambda i: (i, 0))


def _in_proj_kernel(x_ref, g_ref, sc_ref, sh_ref, wrw_ref, wgt_ref, wml_ref, orw_ref, ogt_ref, oml_ref):
    h = _rms(x_ref[...], g_ref[...]) * (1.0 + sc_ref[0]) + sh_ref[0]
    hb = h.astype(BF16)
    orw_ref[...] = _dg(hb, wrw_ref[...], _NN)
    ogt_ref[...] = _dg(hb, wgt_ref[...], _NN)
    oml_ref[...] = _dg(hb, wml_ref[...], _NN)


def _in_proj(x, g, sc, sh, w_rw, w_gate, w_mla, tm, tiles_per_group):
    n = x.shape[0]
    return pl.pallas_call(
        _in_proj_kernel,
        out_shape=(jax.ShapeDtypeStruct((n, RW_COLS), F32),
                   jax.ShapeDtypeStruct((n, 2 * D_MODEL), F32),
                   jax.ShapeDtypeStruct((n, MLA_PAD), F32)),
        grid=(n // tm,),
        in_specs=[_rows(tm, D_MODEL), _full(g), _mod_spec(sc, tm, tiles_per_group),
                  _mod_spec(sh, tm, tiles_per_group), _full(w_rw), _full(w_gate), _full(w_mla)],
        out_specs=(_rows(tm, RW_COLS), _rows(tm, 2 * D_MODEL), _rows(tm, MLA_PAD)),
        compiler_params=_cparams(("arbitrary",)),
        name="in_proj",
    )(x, g, sc, sh, w_rw, w_gate, w_mla)


def _rw_prep_kernel(p_ref, prev_ref, mu_ref, wup_ref, w0_ref, aup_ref, a0_ref, gup_ref, kk_ref, ka_ref,
                    ones_ref, r_ref, lw_ref, k_ref, v_ref, kkn_ref, a_ref, g_ref):
    p = p_ref[...]
    m = p + (prev_ref[...] - p) * mu_ref[...]
    r = m[:, 0:RW_DIM]
    k = m[:, RW_DIM:2 * RW_DIM]
    v = m[:, 2 * RW_DIM:3 * RW_DIM]
    xwa = m[:, 3 * RW_DIM:3 * RW_DIM + LORA_W + LORA_A]
    xg = m[:, 3 * RW_DIM + LORA_W + LORA_A:RW_COLS]
    lw = -DECAY_SCALE * _sigmoid(w0_ref[...] + _dot1(jnp.tanh(xwa), wup_ref[...]))
    a = _sigmoid(a0_ref[...] + _dot1(xwa, aup_ref[...]))
    g = _dot1(_sigmoid(xg), gup_ref[...])
    kk = k * kk_ref[...]
    ss = _dot_exact_rhs(kk * kk, ones_ref[...])
    kk = kk / jnp.maximum(jnp.sqrt(ss), 1e-12)
    r_ref[...] = r
    lw_ref[...] = lw
    k_ref[...] = k * (1.0 + (a - 1.0) * ka_ref[...])
    v_ref[...] = v
    kkn_ref[...] = kk
    a_ref[...] = a
    g_ref[...] = g


def _rw_prep(p_rw, prev, mu, wup, w0, aup, a0, gup, k_k, k_a, ones_blk, tm):
    n = p_rw.shape[0]
    out = jax.ShapeDtypeStruct((n, RW_DIM), F32)
    consts = (mu, wup, w0, aup, a0, gup, k_k, k_a, ones_blk)
    return pl.pallas_call(
        _rw_prep_kernel,
        out_shape=(out,) * 7,
        grid=(n // tm,),
        in_specs=[_rows(tm, RW_COLS), _rows(tm, RW_COLS)] + [_full(c) for c in consts],
        out_specs=(_rows(tm, RW_DIM),) * 7,
        compiler_params=_cparams(("arbitrary",)),
        name="rw_prep",
    )(p_rw, prev, *consts)


def _rw_scan_kernel(r_ref, lw_ref, k_ref, v_ref, kk_ref, a_ref, h0_ref, y_ref, hT_ref, h_scr, *, chunk):
    c = pl.program_id(1)
    nc = pl.num_programs(1)
    C = chunk

    @pl.when(c == 0)
    def _():
        h_scr[...] = h0_ref[0]

    ti = lax.broadcasted_iota(jnp.int32, (C, C), 0)
    si = lax.broadcasted_iota(jnp.int32, (C, C), 1)
    strict = ti > si
    incl = ti >= si
    ltri = jnp.where(incl, 1.0, 0.0).astype(BF16)
    eye_c = jnp.where(ti == si, 1.0, 0.0).astype(F32)
    lane = lax.broadcasted_iota(jnp.int32, (1, LANES), 1)
    head_mask = [jnp.where(lane < RW_HEAD, 1.0, 0.0).astype(F32), jnp.where(lane >= RW_HEAD, 1.0, 0.0).astype(F32)]
    rj = lax.broadcasted_iota(jnp.int32, (LANES, LANES), 0)
    cj = lax.broadcasted_iota(jnp.int32, (LANES, LANES), 1)
    blockdiag = (rj < RW_HEAD) == (cj < RW_HEAD)
    eye_l = rj == cj
    levels = []
    s = 1
    while s < C:
        levels.append((ti // (2 * s) == si // (2 * s)) & ((ti // s) % 2 == 1) & ((si // s) % 2 == 0))
        s *= 2

    for p in range(N_PAIRS):
        sl = slice(p * LANES, (p + 1) * LANES)
        r = r_ref[:, sl]
        lw = lw_ref[:, sl]
        k = k_ref[:, sl]
        v = v_ref[:, sl]
        kk = kk_ref[:, sl]
        a = a_ref[:, sl]
        H = h_scr[p]

        cum = _dot_exact_lhs(ltri, lw)
        cum_last = cum[C - 1:C, :]
        inv_g = jnp.exp(-cum)
        at = -kk * jnp.exp(cum - lw)
        rt = r * jnp.exp(cum)
        beta = kk * a
        bh = beta * inv_g
        kh = k * inv_g
        to_end = jnp.exp(cum_last - cum)
        bt = beta * to_end
        kt = k * to_end
        x_ar = jnp.concatenate([at, rt], axis=0)

        at_p = jnp.zeros((C, LANES), F32)
        u0_p = jnp.zeros((C, LANES), F32)
        rp_p = jnp.zeros((C, LANES), F32)
        y0_p = jnp.zeros((C, LANES), F32)
        for hh in range(2):
            mk = head_mask[hh]
            g_b = _dot3(x_ar, bh * mk, _NT)
            g_k = _dot3(x_ar, kh * mk, _NT)
            a_ab = jnp.where(strict, g_b[:C], 0.0)
            a_rb = jnp.where(incl, g_b[C:], 0.0)
            a_ak = jnp.where(strict, g_k[:C], 0.0)
            a_rk = jnp.where(incl, g_k[C:], 0.0)
            T = eye_c + jnp.where(levels[0], a_ab, 0.0)
            for lm in levels[1:]:
                T = T + _dot3(_dot3(T, jnp.where(lm, a_ab, 0.0)), T)
            x12 = _dot3(jnp.concatenate([a_ak, a_rk], axis=0), v)
            tu = _dot3(T, jnp.concatenate([at, x12[:C]], axis=1))
            ry = _dot3(a_rb, tu)
            at_p = at_p + mk * tu[:, :LANES]
            u0_p = u0_p + mk * tu[:, LANES:]
            rp_p = rp_p + mk * (rt + ry[:, :LANES])
            y0_p = y0_p + mk * (ry[:, LANES:] + x12[C:])

        y_ref[:, sl] = y0_p + _dot3(rp_p, H)
        lhs_t = jnp.concatenate([bt, kt], axis=0).T
        rhs = jnp.concatenate([jnp.concatenate([at_p, u0_p], axis=1),
                               jnp.concatenate([jnp.zeros((C, LANES), F32), v], axis=1)], axis=0)
        pq = _dot3(lhs_t, rhs)
        decay_diag = jnp.where(eye_l, jnp.broadcast_to(jnp.exp(cum_last), (LANES, LANES)), 0.0)
        P = jnp.where(blockdiag, pq[:, :LANES], 0.0) + decay_diag
        Q = jnp.where(blockdiag, pq[:, LANES:], 0.0)
        h_scr[p] = _dot3(P, H) + Q

    @pl.when(c == nc - 1)
    def _():
        hT_ref[0] = h_scr[...]


def _rw_scan(r, lw, k, v, kk, a, h0, batch, chunk):
    n = r.shape[0]
    nc = n // (batch * chunk)
    row = pl.BlockSpec((chunk, RW_DIM), lambda b, c: (b * nc + c, 0))
    hspec = pl.BlockSpec((1, N_PAIRS, LANES, LANES), lambda b, c: (b, 0, 0, 0))
    return pl.pallas_call(
        functools.partial(_rw_scan_kernel, chunk=chunk),
        out_shape=(jax.ShapeDtypeStruct((n, RW_DIM), F32),
                   jax.ShapeDtypeStruct((batch, N_PAIRS, LANES, LANES), F32)),
        grid=(batch, nc),
        in_specs=[row] * 6 + [hspec],
        out_specs=(row, hspec),
        scratch_shapes=[pltpu.VMEM((N_PAIRS, LANES, LANES), F32)],
        compiler_params=_cparams(("arbitrary", "arbitrary")),
        name="rw_scan",
    )(r, lw, k, v, kk, a, h0)


def _mla_prep_kernel(p_ref, qg_ref, wuq_ref, wuk_ref, kvg_ref, cos_ref, sin_ref, rot_ref,
                     ckv_ref, kr_ref, ql_ref, qr_ref):
    p = p_ref[...]
    cq = p[:, 0:Q_LORA]
    ckv = p[:, Q_LORA:Q_LORA + KV_LORA]
    kr = p[:, Q_LORA + KV_LORA:MLA_COLS]
    cos = cos_ref[...]
    sin = sin_ref[...]
    rot = rot_ref[...]
    q = _dot1(_rms(cq, qg_ref[...]), wuq_ref[...])
    q_nope = q[:, 0:MLA_HEADS * QK_NOPE]
    q_rope = q[:, MLA_HEADS * QK_NOPE:]
    q_rope = q_rope * cos + _dot_exact_rhs(q_rope, rot) * sin
    ql_ref[...] = (_dot1(q_nope, wuk_ref[...]) * ATTN_SCALE).astype(BF16)
    qr_ref[...] = (q_rope * ATTN_SCALE).astype(BF16)
    ckv_ref[...] = _rms(ckv, kvg_ref[...])
    kr_ref[...] = kr * cos[:, 0:QK_ROPE] + _dot_exact_rhs(kr, rot[0:QK_ROPE, 0:QK_ROPE]) * sin[:, 0:QK_ROPE]


def _mla_prep(p_mla, q_g, w_uq, w_uk_bd, kv_g, cos, sin, rot, tm):
    n = p_mla.shape[0]
    tab_tiles = cos.shape[0] // tm
    tab = pl.BlockSpec((tm, MLA_HEADS * QK_ROPE), lambda i: (i % tab_tiles, 0))
    return pl.pallas_call(
        _mla_prep_kernel,
        out_shape=(jax.ShapeDtypeStruct((n, KV_LORA), F32),
                   jax.ShapeDtypeStruct((n, QK_ROPE), F32),
                   jax.ShapeDtypeStruct((n, MLA_HEADS * KV_LORA), BF16),
                   jax.ShapeDtypeStruct((n, MLA_HEADS * QK_ROPE), BF16)),
        grid=(n // tm,),
        in_specs=[_rows(tm, MLA_PAD), _full(q_g), _full(w_uq), _full(w_uk_bd), _full(kv_g), tab, tab, _full(rot)],
        out_specs=(_rows(tm, KV_LORA), _rows(tm, QK_ROPE), _rows(tm, MLA_HEADS * KV_LORA),
                   _rows(tm, MLA_HEADS * QK_ROPE)),
        compiler_params=_cparams(("arbitrary",)),
        name="mla_prep",
    )(p_mla, q_g, w_uq, w_uk_bd, kv_g, cos, sin, rot)


FLASH_TQ = 128
FLASH_TK = 512


def _flash_kernel(q_ref, k_ref, o_ref, m_scr, l_scr, acc_scr, *, tq, tk):
    i = pl.program_id(1)
    j = pl.program_id(2)
    nk = pl.num_programs(2)
    rows = tq * MLA_HEADS

    @pl.when(j == 0)
    def _():
        m_scr[...] = jnp.full((rows, 1), -jnp.inf, F32)
        l_scr[...] = jnp.zeros((rows, 1), F32)
        acc_scr[...] = jnp.zeros((rows, KV_LORA), F32)

    q_lo = i * tq
    k_lo = j * tk

    def update(masked):
        k = k_ref[0]
        s = _dg(q_ref[0], k, _NT)
        if masked:
            q_pos = q_lo + lax.broadcasted_iota(jnp.int32, (rows, tk), 0) // MLA_HEADS
            k_pos = k_lo + lax.broadcasted_iota(jnp.int32, (rows, tk), 1)
            s = jnp.where(k_pos <= q_pos, s, -jnp.inf)
        m_old = m_scr[...]
        m_new = jnp.maximum(m_old, jnp.max(s, axis=-1, keepdims=True))
        pe = jnp.exp(s - m_new)
        corr = jnp.exp(m_old - m_new)
        l_scr[...] = l_scr[...] * corr + jnp.sum(pe, axis=-1, keepdims=True)
        acc_scr[...] = acc_scr[...] * corr + _dg(pe.astype(BF16), k[:, 0:KV_LORA], _NN)
        m_scr[...] = m_new

    needed = k_lo <= q_lo + tq - 1
    diag = k_lo + tk - 1 > q_lo

    @pl.when(needed & diag)
    def _():
        update(True)

    @pl.when(needed & jnp.logical_not(diag))
    def _():
        update(False)

    @pl.when(j == nk - 1)
    def _():
        o_ref[0] = (acc_scr[...] / l_scr[...]).astype(o_ref.dtype)


def _flash_attention(q, k, tq, tk):
    b, th, _ = q.shape
    t = k.shape[1]
    rows = tq * MLA_HEADS
    nq, nk = t // tq, t // tk
    last = lambda i: (i * tq + tq - 1) // tk
    return pl.pallas_call(
        functools.partial(_flash_kernel, tq=tq, tk=tk),
        out_shape=jax.ShapeDtypeStruct((b, th, KV_LORA), BF16),
        grid=(b, nq, nk),
        in_specs=[pl.BlockSpec((1, rows, QK_DIM), lambda bb, i, j: (bb, i, 0)),
                  pl.BlockSpec((1, tk, QK_DIM), lambda bb, i, j: (bb, jnp.minimum(j, last(i)), 0))],
        out_specs=pl.BlockSpec((1, rows, KV_LORA), lambda bb, i, j: (bb, i, 0)),
        scratch_shapes=[pltpu.VMEM((rows, 1), F32), pltpu.VMEM((rows, 1), F32), pltpu.VMEM((rows, KV_LORA), F32)],
        compiler_params=_cparams(("arbitrary", "arbitrary", "arbitrary")),
        name="mla_flash",
    )(q, k)


PAGES_PER_CHUNK = 16


def _paged_kernel(pt_ref, q_ref, kn_ref, ckv_hbm, kr_hbm, o_ref, ckv_buf, kr_buf, sem, *, layer, n_pages, t_new):
    b = pl.program_id(0)
    pc = ckv_buf.shape[1]
    n_chunks = n_pages // pc
    rows = q_ref.shape[1]

    def copies(chunk, slot, p):
        page = pt_ref[b, chunk * pc + p]
        return (pltpu.make_async_copy(ckv_hbm.at[layer, page], ckv_buf.at[slot, p], sem.at[0, slot]),
                pltpu.make_async_copy(kr_hbm.at[layer, page], kr_buf.at[slot, p], sem.at[1, slot]))

    def start(chunk, slot):
        for p in range(pc):
            for cp in copies(chunk, slot, p):
                cp.start()

    def wait(chunk, slot):
        for p in range(pc):
            for cp in copies(chunk, slot, p):
                cp.wait()

    q = q_ref[0]
    q_lat = q[:, 0:KV_LORA]
    q_rope = q[:, KV_LORA:QK_DIM]

    def online(carry, s, vals):
        m_old, l_old, acc = carry
        m_new = jnp.maximum(m_old, jnp.max(s, axis=-1, keepdims=True))
        pe = jnp.exp(s - m_new)
        corr = jnp.exp(m_old - m_new)
        return (m_new, l_old * corr + jnp.sum(pe, axis=-1, keepdims=True),
                acc * corr + _dg(pe.astype(BF16), vals, _NN))

    start(0, 0)

    def body(chunk, carry):
        slot = chunk % 2

        @pl.when(chunk + 1 < n_chunks)
        def _():
            start(chunk + 1, 1 - slot)

        wait(chunk, slot)
        kc = ckv_buf[slot].reshape(pc * PAGE_SIZE, KV_LORA).astype(BF16)
        kp = kr_buf[slot].reshape(pc * PAGE_SIZE, QK_ROPE).astype(BF16)
        s = _dg(q_lat, kc, _NT) + _dg(q_rope, kp, _NT)
        return online(carry, s, kc)

    init = (jnp.full((rows, 1), -jnp.inf, F32), jnp.zeros((rows, 1), F32), jnp.zeros((rows, KV_LORA), F32))
    carry = lax.fori_loop(0, n_chunks, body, init)

    kn = kn_ref[0]
    s = _dg(q, kn, _NT)
    q_t = lax.broadcasted_iota(jnp.int32, s.shape, 0) // MLA_HEADS
    k_t = lax.broadcasted_iota(jnp.int32, s.shape, 1)
    s = jnp.where((k_t <= q_t) & (k_t < t_new), s, -jnp.inf)
    _, l_sum, acc = online(carry, s, kn[:, 0:KV_LORA])
    o_ref[0] = (acc / l_sum).astype(o_ref.dtype)


def _paged_attention(page_table, q, k_new, cache_ckv, cache_kr, layer, t_new):
    b, rows, _ = q.shape
    n_pages = page_table.shape[1]
    pc = math.gcd(PAGES_PER_CHUNK, n_pages)
    kn_rows = k_new.shape[1]
    grid_spec = pltpu.PrefetchScalarGridSpec(
        num_scalar_prefetch=1,
        grid=(b,),
        in_specs=[pl.BlockSpec((1, rows, QK_DIM), lambda i, pt: (i, 0, 0)),
                  pl.BlockSpec((1, kn_rows, QK_DIM), lambda i, pt: (i, 0, 0)),
                  pl.BlockSpec(memory_space=pl.ANY),
                  pl.BlockSpec(memory_space=pl.ANY)],
        out_specs=pl.BlockSpec((1, rows, KV_LORA), lambda i, pt: (i, 0, 0)),
        scratch_shapes=[pltpu.VMEM((2, pc, PAGE_SIZE, KV_LORA), F32),
                        pltpu.VMEM((2, pc, PAGE_SIZE, QK_ROPE), F32),
                        pltpu.SemaphoreType.DMA((2, 2))],
    )
    return pl.pallas_call(
        functools.partial(_paged_kernel, layer=layer, n_pages=n_pages, t_new=t_new),
        out_shape=jax.ShapeDtypeStruct((b, rows, KV_LORA), BF16),
        grid_spec=grid_spec,
        compiler_params=_cparams(("arbitrary",)),
        name="mla_paged",
    )(page_table, q, k_new, cache_ckv, cache_kr)


def _merge_kernel(x_ref, y_ref, r_ref, k_ref, v_ref, g_ref, ol_ref, gate_ref, gt_ref, ng_ref,
                  gnw_ref, gnb_ref, rk_ref, avg_ref, ones_ref, wo_ref, wuv_ref, mwo_ref, wout_ref, o_ref):
    y = y_ref[...]
    mean = _dot_exact_rhs(y, avg_ref[...])
    yc = y - mean
    var = _dot_exact_rhs(yc * yc, avg_ref[...])
    yn = yc * lax.rsqrt(var + GN_EPS) * gnw_ref[...] + gnb_ref[...]
    v = v_ref[...]
    bonus = _dot_exact_rhs(r_ref[...] * k_ref[...] * rk_ref[...], ones_ref[...]) * v
    o_rw = _dot1((yn + bonus) * g_ref[...], wo_ref[...])
    o_mla = _dot1(_dg(ol_ref[...], wuv_ref[...], _NN), mwo_ref[...])
    gate = _sigmoid(gate_ref[...])
    mix = _dot1(gate[:, 0:D_MODEL] * o_rw + gate[:, D_MODEL:] * o_mla, wout_ref[...])
    o_ref[...] = x_ref[...] + gt_ref[0] * _rms(mix, ng_ref[...])


def _merge(x, y, r, k, v, g, o_lat, p_gate, gt, ng, consts, tm, tiles_per_group):
    n = x.shape[0]
    return pl.pallas_call(
        _merge_kernel,
        out_shape=jax.ShapeDtypeStruct((n, D_MODEL), F32),
        grid=(n // tm,),
        in_specs=[_rows(tm, D_MODEL)] + [_rows(tm, RW_DIM)] * 5
                 + [_rows(tm, MLA_HEADS * KV_LORA), _rows(tm, 2 * D_MODEL), _mod_spec(gt, tm, tiles_per_group),
                    _full(ng)] + [_full(c) for c in consts],
        out_specs=_rows(tm, D_MODEL),
        compiler_params=_cparams(("arbitrary",)),
        name="merge",
    )(x, y, r, k, v, g, o_lat, p_gate, gt, ng, *consts)


FFN_CHUNK = 256


def _ffn_kernel(x_ref, g2_ref, sc_ref, sh_ref, gt_ref, g3_ref, win_ref, wout_ref, o_ref, *, d_ff):
    x = x_ref[...]
    hb = (_rms(x, g2_ref[...]) * (1.0 + sc_ref[0]) + sh_ref[0]).astype(BF16)
    acc = jnp.zeros(x.shape, F32)
    for c0 in range(0, d_ff, FFN_CHUNK):
        u = _dg(hb, win_ref[:, c0:c0 + FFN_CHUNK], _NN)
        vv = _dg(hb, win_ref[:, d_ff + c0:d_ff + c0 + FFN_CHUNK], _NN)
        act = (u * _sigmoid(u) * vv).astype(BF16)
        acc = acc + _dg(act, wout_ref[c0:c0 + FFN_CHUNK, :], _NN)
    o_ref[...] = x + gt_ref[0] * _rms(acc, g3_ref[...])


def _ffn(x, g2, sc, sh, gt, g3, w_in, w_out, tm, tiles_per_group):
    n = x.shape[0]
    d_ff = w_out.shape[0]
    ms = lambda m: _mod_spec(m, tm, tiles_per_group)
    return pl.pallas_call(
        functools.partial(_ffn_kernel, d_ff=d_ff),
        out_shape=jax.ShapeDtypeStruct((n, D_MODEL), F32),
        grid=(n // tm,),
        in_specs=[_rows(tm, D_MODEL), _full(g2), ms(sc), ms(sh), ms(gt), _full(g3), _full(w_in), _full(w_out)],
        out_specs=_rows(tm, D_MODEL),
        compiler_params=_cparams(("arbitrary",)),
        name="ffn",
    )(x, g2, sc, sh, gt, g3, w_in, w_out)


def _head_block_matrix(n_heads, width, value):
    idx = np.arange(n_heads * width) // width
    return jnp.asarray(np.where(idx[:, None] == idx[None, :], value, 0.0), dtype=BF16)


def _rotate_half_matrix(n_heads, width):
    half = width // 2
    m = np.zeros((n_heads * width, n_heads * width), np.float32)
    for h in range(n_heads):
        for i in range(half):
            m[h * width + half + i, h * width + i] = -1.0
            m[h * width + i, h * width + half + i] = 1.0
    return jnp.asarray(m, dtype=BF16)


def _rope_tables(pos, reps):
    half = QK_ROPE // 2
    inv = ROPE_THETA ** (-jnp.arange(half, dtype=F32) / half)
    ang = pos.astype(F32)[:, None] * inv[None, :]
    cos = jnp.tile(jnp.cos(ang), (1, 2 * reps))
    sin = jnp.tile(jnp.sin(ang), (1, 2 * reps))
    return cos, sin


def _block_diag_heads(w):
    h, a, b = w.shape
    eye = jnp.eye(h, dtype=w.dtype)
    return (eye[:, None, :, None] * w[:, :, None, :]).reshape(h * a, h * b)


def _state_to_pairs(s):
    b = s.shape[0]
    st = jnp.swapaxes(s, -1, -2).reshape(b, N_PAIRS, 2, RW_HEAD, RW_HEAD)
    eye = jnp.eye(2, dtype=s.dtype)
    return (st[:, :, :, :, None, :] * eye[None, None, :, None, :, None]).reshape(b, N_PAIRS, LANES, LANES)


def _pairs_to_state(h):
    b = h.shape[0]
    h6 = h.reshape(b, N_PAIRS, 2, RW_HEAD, 2, RW_HEAD)
    st = jnp.stack([h6[:, :, 0, :, 0, :], h6[:, :, 1, :, 1, :]], axis=2)
    return jnp.swapaxes(st, -1, -2).reshape(b, RW_HEADS, RW_HEAD, RW_HEAD)


def _prep_weights(P, layer):
    w = {}
    w_in = P['w_in'][layer]
    w['w_rw'] = w_in[:, :RW_COLS].astype(BF16)
    w['w_mla'] = jnp.pad(w_in[:, RW_COLS:RW_COLS + MLA_COLS], ((0, 0), (0, MLA_PAD - MLA_COLS))).astype(BF16)
    w['w_gate'] = w_in[:, RW_COLS + MLA_COLS:].astype(BF16)
    row = lambda z: z.reshape(1, -1)
    w['mu'] = row(P['rw_mu'][layer])
    w['wup'] = jnp.pad(P['rw_w_up'][layer], ((0, LORA_A), (0, 0))).astype(BF16)
    w['aup'] = jnp.pad(P['rw_a_up'][layer], ((LORA_W, 0), (0, 0))).astype(BF16)
    w['gup'] = P['rw_g_up'][layer].astype(BF16)
    for name in ('rw_w0', 'rw_a0', 'rw_k_k', 'rw_k_a', 'rw_gn_w', 'rw_gn_b', 'mla_q_g', 'mla_kv_g'):
        w[name] = row(P[name][layer])
    w['rw_r_k'] = P['rw_r_k'][layer].reshape(1, RW_DIM)
    w['rw_w_o'] = P['rw_w_o'][layer].astype(BF16)
    uq = P['mla_w_uq'][layer].reshape(Q_LORA, MLA_HEADS, QK_NOPE + QK_ROPE)
    w['w_uq'] = jnp.concatenate([uq[:, :, :QK_NOPE].reshape(Q_LORA, -1), uq[:, :, QK_NOPE:].reshape(Q_LORA, -1)],
                                axis=1).astype(BF16)
    w['w_uk'] = _block_diag_heads(jnp.transpose(P['mla_w_uk'][layer], (1, 2, 0))).astype(BF16)
    w['w_uv'] = _block_diag_heads(jnp.transpose(P['mla_w_uv'][layer], (1, 0, 2))).astype(BF16)
    w['mla_w_o'] = P['mla_w_o'][layer].astype(BF16)
    w['w_out'] = P['w_out'][layer].astype(BF16)
    w['ffn_w_in'] = P['ffn_w_in'][layer].astype(BF16)
    w['ffn_w_out'] = P['ffn_w_out'][layer].astype(BF16)
    w['norm_g'] = [row(P['norm_g'][layer][i]) for i in range(4)]
    return w


def _layer(layer, x, c, pos, shift0, wkv0, P, W, consts, attend):
    b, t, _ = x.shape
    n = b * t
    tm = min(256, n)
    xf = x.reshape(n, D_MODEL)
    mod = _mod_matmul(c, P['mod_w'][layer], P['mod_b'][layer])
    mods = jnp.split(mod, 6, axis=-1)
    if t % tm == 0:
        mods = [z.reshape(b, 1, D_MODEL) for z in mods]
        tiles_per_group = t // tm
    else:
        mods = [jnp.repeat(z, t, axis=0).reshape(n // tm, tm, D_MODEL) for z in mods]
        tiles_per_group = 1
    sh1, sc1, gt1, sh2, sc2, gt2 = mods
    ng = W['norm_g']

    p_rw, p_gate, p_mla = _in_proj(xf, ng[0], sc1, sh1, W['w_rw'], W['w_gate'], W['w_mla'], tm, tiles_per_group)

    p3 = p_rw.reshape(b, t, RW_COLS)
    prev = jnp.concatenate([shift0[:, None, :], p3[:, :-1]], axis=1).reshape(n, RW_COLS)
    shift_new = p3[:, -1]
    r, lw, k, v, kk, a, g = _rw_prep(p_rw, prev, W['mu'], W['wup'], W['rw_w0'], W['aup'], W['rw_a0'], W['gup'],
                                     W['rw_k_k'], W['rw_k_a'], consts['ones_blk'], tm)
    chunk = 64 if t % 64 == 0 else 8
    t_pad = -(-t // chunk) * chunk
    scan_in = [r, lw, k, v, kk, a]
    if t_pad != t:
        scan_in = [jnp.pad(z.reshape(b, t, RW_DIM), ((0, 0), (0, t_pad - t), (0, 0))).reshape(b * t_pad, RW_DIM)
                   for z in scan_in]
    y, h_t = _rw_scan(*scan_in, _state_to_pairs(wkv0), b, chunk)
    if t_pad != t:
        y = y.reshape(b, t_pad, RW_DIM)[:, :t].reshape(n, RW_DIM)
    wkv_new = _pairs_to_state(h_t)

    cos, sin = consts['rope'](pos, b, t, tm)
    ckv, kr, q_lat, q_rope = _mla_prep(p_mla, W['mla_q_g'], W['w_uq'], W['w_uk'], W['mla_kv_g'], cos, sin,
                                       consts['rot'], tm)
    q_cat = jnp.concatenate([q_lat.reshape(n, MLA_HEADS, KV_LORA), q_rope.reshape(n, MLA_HEADS, QK_ROPE)],
                            axis=-1).reshape(b, t * MLA_HEADS, QK_DIM)
    k_cat = jnp.concatenate([ckv, kr], axis=-1).astype(BF16).reshape(b, t, QK_DIM)
    o_lat = attend(layer, q_cat, k_cat).reshape(n, MLA_HEADS * KV_LORA)

    merge_consts = (W['rw_gn_w'], W['rw_gn_b'], W['rw_r_k'], consts['avg_blk'], consts['ones_blk'],
                    W['rw_w_o'], W['w_uv'], W['mla_w_o'], W['w_out'])
    x1 = _merge(xf, y, r, k, v, g, o_lat, p_gate, gt1, ng[1], merge_consts, tm, tiles_per_group)
    x2 = _ffn(x1, ng[2], sc2, sh2, gt2, ng[3], W['ffn_w_in'], W['ffn_w_out'], tm, tiles_per_group)
    return (x2.reshape(b, t, D_MODEL), ckv.reshape(b, t, KV_LORA), kr.reshape(b, t, QK_ROPE), wkv_new, shift_new)


def _trunk(x, c, pos, shift0s, wkv0s, P, Ws, consts, attend):
    depth = shift0s.shape[0]
    ckvs, krs, wkvs, shifts = [], [], [], []
    for layer in range(depth):
        x, ckv, kr, wkv, sh = _layer(layer, x, c, pos, shift0s[layer], wkv0s[layer], P, Ws[layer], consts, attend)
        ckvs.append(ckv)
        krs.append(kr)
        wkvs.append(wkv)
        shifts.append(sh)
    return x, jnp.stack(ckvs), jnp.stack(krs), jnp.stack(wkvs), jnp.stack(shifts)


def kernel(x_prompt, x_sample, cache_ckv, cache_kr, state_wkv, state_shift, page_table, c_prompt, c_sample, mod_w, mod_b, norm_g, w_in, rw_mu, rw_w_up, rw_w0, rw_a_up, rw_a0, rw_g_up, rw_k_k, rw_k_a, rw_r_k, rw_gn_w, rw_gn_b, rw_w_o, mla_q_g, mla_w_uq, mla_kv_g, mla_w_uk, mla_w_uv, mla_w_o, w_out, ffn_w_in, ffn_w_out):
    P = dict(mod_w=mod_w, mod_b=mod_b, norm_g=norm_g, w_in=w_in, rw_mu=rw_mu, rw_w_up=rw_w_up,
             rw_w0=rw_w0, rw_a_up=rw_a_up, rw_a0=rw_a0, rw_g_up=rw_g_up, rw_k_k=rw_k_k,
             rw_k_a=rw_k_a, rw_r_k=rw_r_k, rw_gn_w=rw_gn_w, rw_gn_b=rw_gn_b, rw_w_o=rw_w_o,
             mla_q_g=mla_q_g, mla_w_uq=mla_w_uq, mla_kv_g=mla_kv_g, mla_w_uk=mla_w_uk,
             mla_w_uv=mla_w_uv, mla_w_o=mla_w_o, w_out=w_out, ffn_w_in=ffn_w_in,
             ffn_w_out=ffn_w_out)
    depth = w_in.shape[0]
    Ws = [_prep_weights(P, layer) for layer in range(depth)]

    def rope(pos, b, t, tm):
        cos, sin = _rope_tables(pos, MLA_HEADS)
        if t % tm != 0:
            cos, sin = jnp.tile(cos, (b, 1)), jnp.tile(sin, (b, 1))
        return cos, sin

    consts = dict(ones_blk=_head_block_matrix(RW_HEADS, RW_HEAD, 1.0),
                  avg_blk=_head_block_matrix(RW_HEADS, RW_HEAD, 1.0 / RW_HEAD),
                  rot=_rotate_half_matrix(MLA_HEADS, QK_ROPE),
                  rope=rope)

    bp, tp, _ = x_prompt.shape
    zero_shift = jnp.zeros((depth, bp, RW_COLS), x_prompt.dtype)
    zero_wkv = jnp.zeros((depth, bp, RW_HEADS, RW_HEAD, RW_HEAD), state_wkv.dtype)

    def attend_prompt(layer, q_cat, k_cat):
        return _flash_attention(q_cat, k_cat, min(FLASH_TQ, tp), min(FLASH_TK, tp))

    y_prompt, ckv_p, kr_p, wkv_p, shift_p = _trunk(
        x_prompt, c_prompt, jnp.arange(tp), zero_shift, zero_wkv, P, Ws, consts, attend_prompt)

    bs, ts, _ = x_sample.shape
    past = page_table.shape[1] * PAGE_SIZE

    def attend_sample(layer, q_cat, k_cat):
        k_new = jnp.pad(k_cat, ((0, 0), (0, 8 - ts), (0, 0)))
        return _paged_attention(page_table, q_cat, k_new, cache_ckv, cache_kr, layer, ts)

    y_sample, ckv_s, kr_s, wkv_s, shift_s = _trunk(
        x_sample, c_sample, past + jnp.arange(ts), state_shift, state_wkv, P, Ws, consts, attend_sample)
    return (y_prompt, y_sample, ckv_p, kr_p, wkv_p, shift_p, ckv_s, kr_s, wkv_s, shift_s)
```

```python
import functools
import math

import jax
import jax.numpy as jnp
import numpy as np
from jax import lax
from jax.experimental import pallas as pl
from jax.experimental.pallas import tpu as pltpu

F32 = jnp.float32
BF16 = jnp.bfloat16

D_MODEL = 1024
RW_HEADS = 8
RW_HEAD = 64
RW_DIM = RW_HEADS * RW_HEAD
LORA_W = 64
LORA_A = 64
LORA_G = 128
RW_COLS = 3 * RW_DIM + LORA_W + LORA_A + LORA_G
MLA_HEADS = 8
QK_NOPE = 64
QK_ROPE = 32
V_HEAD = 64
Q_LORA = 384
KV_LORA = 128
MLA_COLS = Q_LORA + KV_LORA + QK_ROPE
MLA_PAD = 640
QK_DIM = KV_LORA + QK_ROPE
PAGE_SIZE = 128

ROPE_THETA = 10000.0
NORM_EPS = 1e-6
GN_EPS = 64e-5
DECAY_SCALE = math.exp(-0.5)
ATTN_SCALE = (QK_NOPE + QK_ROPE) ** -0.5
LOG2_E = math.log2(math.e)

LANES = 128
N_PAIRS = RW_HEADS // 2
VMEM_LIMIT = 56 * 1024 * 1024


def _cparams(sem):
    return pltpu.CompilerParams(dimension_semantics=sem, vmem_limit_bytes=VMEM_LIMIT)


def _dg(a, b, dims):
    return lax.dot_general(a, b, (dims, ((), ())), preferred_element_type=F32)


_NN = ((1,), (0,))
_NT = ((1,), (1,))


def _dot1(a, b, dims=_NN):
    return _dg(a.astype(BF16), b.astype(BF16), dims)


def _split2(a):
    hi = a.astype(BF16)
    lo = (a - hi.astype(F32)).astype(BF16)
    return hi, lo


def _split3(a):
    hi = a.astype(BF16)
    r = a - hi.astype(F32)
    mid = r.astype(BF16)
    lo = (r - mid.astype(F32)).astype(BF16)
    return hi, mid, lo


def _dot3(a, b, dims=_NN):
    a1, a2 = _split2(a)
    b1, b2 = _split2(b)
    return _dg(a1, b1, dims) + (_dg(a1, b2, dims) + _dg(a2, b1, dims))


def _dot_exact_rhs(a, b_bf16):
    a1, a2, a3 = _split3(a)
    return _dg(a1, b_bf16, _NN) + (_dg(a2, b_bf16, _NN) + _dg(a3, b_bf16, _NN))


def _dot_exact_lhs(a_bf16, b):
    b1, b2, b3 = _split3(b)
    return _dg(a_bf16, b1, _NN) + (_dg(a_bf16, b2, _NN) + _dg(a_bf16, b3, _NN))


def _rms(x, g):
    return x * lax.rsqrt(jnp.mean(x * x, axis=-1, keepdims=True) + NORM_EPS) * g


def _sigmoid(x):
    return 1.0 / (1.0 + jnp.exp(-x))


def _mod_kernel(c_ref, w_ref, b_ref, o_ref):
    o_ref[...] = _dot3(c_ref[...], w_ref[0]) + b_ref[0]


def _mod_matmul(c, w, b, layer):
    m, k = c.shape
    depth, _, n = w.shape
    tn = 1024
    return pl.pallas_call(
        _mod_kernel,
        out_shape=jax.ShapeDtypeStruct((m, n), F32),
        grid=(n // tn,),
        in_specs=[pl.BlockSpec((m, k), lambda j: (0, 0)),
                  pl.BlockSpec((1, k, tn), lambda j: (layer, 0, j)),
                  pl.BlockSpec((1, 1, tn), lambda j: (layer, 0, j))],
        out_specs=pl.BlockSpec((m, tn), lambda j: (0, j)),
        compiler_params=_cparams(("arbitrary",)),
        name="mod_matmul",
    )(c, w, b.reshape(depth, 1, n))


def _mod_spec(mod, tm, tiles_per_group):
    g, r, d = mod.shape
    if r == 1:
        return pl.BlockSpec((1, 1, d), lambda i: (i // tiles_per_group, 0, 0))
    return pl.BlockSpec((1, r, d), lambda i: (i, 0, 0))


def _full(a):
    nd = a.ndim
    return pl.BlockSpec(a.shape, lambda i: (0,) * nd)


def _rows(tm, width):
    return pl.BlockSpec((tm, width), lambda i: (i, 0))


def _in_proj_kernel(x_ref, g_ref, sc_ref, sh_ref, wrw_ref, wgt_ref, wml_ref, orw_ref, ogt_ref, oml_ref):
    h = _rms(x_ref[...], g_ref[...]) * (1.0 + sc_ref[0]) + sh_ref[0]
    hb = h.astype(BF16)
    orw_ref[...] = _dg(hb, wrw_ref[...], _NN)
    ogt_ref[...] = _dg(hb, wgt_ref[...], _NN)
    oml_ref[...] = _dg(hb, wml_ref[...], _NN)


def _in_proj(x, g, sc, sh, w_rw, w_gate, w_mla, tm, tiles_per_group):
    n = x.shape[0]
    return pl.pallas_call(
        _in_proj_kernel,
        out_shape=(jax.ShapeDtypeStruct((n, RW_COLS), F32),
                   jax.ShapeDtypeStruct((n, 2 * D_MODEL), F32),
                   jax.ShapeDtypeStruct((n, MLA_PAD), F32)),
        grid=(n // tm,),
        in_specs=[_rows(tm, D_MODEL), _full(g), _mod_spec(sc, tm, tiles_per_group),
                  _mod_spec(sh, tm, tiles_per_group), _full(w_rw), _full(w_gate), _full(w_mla)],
        out_specs=(_rows(tm, RW_COLS), _rows(tm, 2 * D_MODEL), _rows(tm, MLA_PAD)),
        compiler_params=_cparams(("arbitrary",)),
        name="in_proj",
    )(x, g, sc, sh, w_rw, w_gate, w_mla)


def _rw_prep_kernel(p_ref, s0_ref, mu_ref, wup_ref, w0_ref, aup_ref, a0_ref, gup_ref, kk_ref, ka_ref,
                    ones_ref, r_ref, lw_ref, k_ref, v_ref, kkn_ref, a_ref, g_ref, carry_scr, *, seq_len):
    i = pl.program_id(0)
    p = p_ref[...]
    tm = p.shape[0]

    @pl.when(i == 0)
    def _():
        carry_scr[...] = jnp.zeros(carry_scr.shape, F32)

    row = lax.broadcasted_iota(jnp.int32, (tm, 1), 0)
    above = jnp.where(row == 0, carry_scr[...], pltpu.roll(p, 1, 0))
    prev = jnp.where((i * tm + row) % seq_len == 0, s0_ref[0], above)
    carry_scr[...] = p[tm - 1:tm, :]
    m = p + (prev - p) * mu_ref[...]
    r = m[:, 0:RW_DIM]
    k = m[:, RW_DIM:2 * RW_DIM]
    v = m[:, 2 * RW_DIM:3 * RW_DIM]
    xwa = m[:, 3 * RW_DIM:3 * RW_DIM + LORA_W + LORA_A]
    xg = m[:, 3 * RW_DIM + LORA_W + LORA_A:RW_COLS]
    lw = -DECAY_SCALE * _sigmoid(w0_ref[...] + _dot1(jnp.tanh(xwa), wup_ref[...]))
    a = _sigmoid(a0_ref[...] + _dot1(xwa, aup_ref[...]))
    g = _dot1(_sigmoid(xg), gup_ref[...])
    kk = k * kk_ref[...]
    ss = _dot_exact_rhs(kk * kk, ones_ref[...])
    kk = kk / jnp.maximum(jnp.sqrt(ss), 1e-12)
    r_ref[...] = r
    lw_ref[...] = lw
    k_ref[...] = k * (1.0 + (a - 1.0) * ka_ref[...])
    v_ref[...] = v
    kkn_ref[...] = kk
    a_ref[...] = a
    g_ref[...] = g


def _rw_prep(p_rw, shift0, seq_len, mu, wup, w0, aup, a0, gup, k_k, k_a, ones_blk, tm, tiles_per_group):
    n = p_rw.shape[0]
    out = jax.ShapeDtypeStruct((n, RW_DIM), F32)
    consts = (mu, wup, w0, aup, a0, gup, k_k, k_a, ones_blk)
    return pl.pallas_call(
        functools.partial(_rw_prep_kernel, seq_len=seq_len),
        out_shape=(out,) * 7,
        grid=(n // tm,),
        in_specs=[_rows(tm, RW_COLS), _mod_spec(shift0, tm, tiles_per_group)] + [_full(c) for c in consts],
        out_specs=(_rows(tm, RW_DIM),) * 7,
        scratch_shapes=[pltpu.VMEM((1, RW_COLS), F32)],
        compiler_params=_cparams(("arbitrary",)),
        name="rw_prep",
    )(p_rw, shift0, *consts)


def _rw_scan_kernel(r_ref, lw_ref, k_ref, v_ref, kk_ref, a_ref, h0_ref, y_ref, hT_ref, h_scr, *, chunk, bb):
    c = pl.program_id(1)
    nc = pl.num_programs(1)
    C = chunk

    @pl.when(c == 0)
    def _():
        h_scr[...] = h0_ref[...]

    ti = lax.broadcasted_iota(jnp.int32, (C, C), 0)
    si = lax.broadcasted_iota(jnp.int32, (C, C), 1)
    strict = ti > si
    incl = ti >= si
    ltri = jnp.where(incl, 1.0, 0.0).astype(BF16)
    eye_c = jnp.where(ti == si, 1.0, 0.0).astype(F32)
    lane = lax.broadcasted_iota(jnp.int32, (1, LANES), 1)
    head_mask = [jnp.where(lane < RW_HEAD, 1.0, 0.0).astype(F32), jnp.where(lane >= RW_HEAD, 1.0, 0.0).astype(F32)]
    rj = lax.broadcasted_iota(jnp.int32, (LANES, LANES), 0)
    cj = lax.broadcasted_iota(jnp.int32, (LANES, LANES), 1)
    blockdiag = (rj < RW_HEAD) == (cj < RW_HEAD)
    eye_l = rj == cj
    levels = []
    s = 1
    while s < C:
        levels.append((ti // (2 * s) == si // (2 * s)) & ((ti // s) % 2 == 1) & ((si // s) % 2 == 0))
        s *= 2
    zeros_cl = jnp.zeros((C, LANES), F32)

    pairs = [(bi, p) for bi in range(bb) for p in range(N_PAIRS)]
    heads = [(q, hh) for q in range(len(pairs)) for hh in range(2)]
    load = lambda ref: [ref[bi, :, p * LANES:(p + 1) * LANES] for bi, p in pairs]
    r, lw, k, v, kk, a = (load(ref) for ref in (r_ref, lw_ref, k_ref, v_ref, kk_ref, a_ref))
    H = [h_scr[bi, p] for bi, p in pairs]

    cum = [_dot_exact_lhs(ltri, x) for x in lw]
    cum_last = [x[C - 1:C, :] for x in cum]
    inv_g = [jnp.exp(-x) for x in cum]
    at = [-kk[q] * jnp.exp(cum[q] - lw[q]) for q in range(len(pairs))]
    rt = [r[q] * jnp.exp(cum[q]) for q in range(len(pairs))]
    beta = [kk[q] * a[q] for q in range(len(pairs))]
    bh = [beta[q] * inv_g[q] for q in range(len(pairs))]
    kh = [k[q] * inv_g[q] for q in range(len(pairs))]
    to_end = [jnp.exp(cum_last[q] - cum[q]) for q in range(len(pairs))]
    x_ar = [jnp.concatenate([at[q], rt[q]], axis=0) for q in range(len(pairs))]

    g_b = [_dot3(x_ar[q], bh[q] * head_mask[hh], _NT) for q, hh in heads]
    g_k = [_dot3(x_ar[q], kh[q] * head_mask[hh], _NT) for q, hh in heads]
    a_ab = [jnp.where(strict, g[:C], 0.0) for g in g_b]
    a_rb = [jnp.where(incl, g[C:], 0.0) for g in g_b]
    a_kk = [jnp.concatenate([jnp.where(strict, g[:C], 0.0), jnp.where(incl, g[C:], 0.0)], axis=0) for g in g_k]
    x12 = [_dot3(a_kk[i], v[q]) for i, (q, hh) in enumerate(heads)]
    T = [eye_c + jnp.where(levels[0], x, 0.0) for x in a_ab]
    for lm in levels[1:]:
        ta = [_dot3(T[i], jnp.where(lm, a_ab[i], 0.0)) for i in range(len(heads))]
        T = [T[i] + _dot3(ta[i], T[i]) for i in range(len(heads))]
    tu = [_dot3(T[i], jnp.concatenate([at[q], x12[i][:C]], axis=1)) for i, (q, hh) in enumerate(heads)]
    ry = [_dot3(a_rb[i], tu[i]) for i in range(len(heads))]

    def combine(f):
        return [head_mask[0] * f(2 * q) + head_mask[1] * f(2 * q + 1) for q in range(len(pairs))]

    at_p = combine(lambda i: tu[i][:, :LANES])
    u0_p = combine(lambda i: tu[i][:, LANES:])
    rp_p = combine(lambda i: ry[i][:, :LANES])
    y0_p = combine(lambda i: ry[i][:, LANES:] + x12[i][C:])

    y = [y0_p[q] + _dot3(rt[q] + rp_p[q], H[q]) for q in range(len(pairs))]
    lhs_t = [jnp.concatenate([beta[q] * to_end[q], k[q] * to_end[q]], axis=0).T for q in range(len(pairs))]
    rhs = [jnp.concatenate([jnp.concatenate([at_p[q], u0_p[q]], axis=1),
                            jnp.concatenate([zeros_cl, v[q]], axis=1)], axis=0) for q in range(len(pairs))]
    pq = [_dot3(lhs_t[q], rhs[q]) for q in range(len(pairs))]
    P = [jnp.where(blockdiag, pq[q][:, :LANES], 0.0)
         + jnp.where(eye_l, jnp.broadcast_to(jnp.exp(cum_last[q]), (LANES, LANES)), 0.0) for q in range(len(pairs))]
    h_new = [_dot3(P[q], H[q]) + jnp.where(blockdiag, pq[q][:, LANES:], 0.0) for q in range(len(pairs))]

    for q, (bi, p) in enumerate(pairs):
        y_ref[bi, :, p * LANES:(p + 1) * LANES] = y[q]
        h_scr[bi, p] = h_new[q]

    @pl.when(c == nc - 1)
    def _():
        hT_ref[...] = h_scr[...]


def _rw_scan(r, lw, k, v, kk, a, h0, chunk, bb):
    b, t, _ = r.shape
    nc = t // chunk
    row = pl.BlockSpec((bb, chunk, RW_DIM), lambda g, c: (g, c, 0))
    hspec = pl.BlockSpec((bb, N_PAIRS, LANES, LANES), lambda g, c: (g, 0, 0, 0))
    return pl.pallas_call(
        functools.partial(_rw_scan_kernel, chunk=chunk, bb=bb),
        out_shape=(jax.ShapeDtypeStruct((b, t, RW_DIM), F32),
                   jax.ShapeDtypeStruct((b, N_PAIRS, LANES, LANES), F32)),
        grid=(b // bb, nc),
        in_specs=[row] * 6 + [hspec],
        out_specs=(row, hspec),
        scratch_shapes=[pltpu.VMEM((bb, N_PAIRS, LANES, LANES), F32)],
        compiler_params=_cparams(("arbitrary", "arbitrary")),
        name="rw_scan",
    )(r, lw, k, v, kk, a, h0)


def _mla_prep_kernel(p_ref, qg_ref, wuq_ref, wuk_ref, kvg_ref, cos_ref, sin_ref, rot_ref, *out_refs, flash_layout):
    p = p_ref[...]
    tm = p.shape[0]
    cq = p[:, 0:Q_LORA]
    ckv = p[:, Q_LORA:Q_LORA + KV_LORA]
    kr = p[:, Q_LORA + KV_LORA:MLA_COLS]
    cos = cos_ref[...]
    sin = sin_ref[...]
    rot = rot_ref[...]
    q = _dot1(_rms(cq, qg_ref[...]), wuq_ref[...])
    q_nope = q[:, 0:MLA_HEADS * QK_NOPE]
    q_rope = q[:, MLA_HEADS * QK_NOPE:]
    q_rope = q_rope * cos + _dot_exact_rhs(q_rope, rot) * sin
    q_scale = ATTN_SCALE * LOG2_E
    q_lat = _dot1(q_nope, wuk_ref[...]) * q_scale
    q_rope = q_rope * q_scale
    ckv_n = _rms(ckv, kvg_ref[...])
    kr_r = kr * cos[:, 0:QK_ROPE] + _dot_exact_rhs(kr, rot[0:QK_ROPE, 0:QK_ROPE]) * sin[:, 0:QK_ROPE]
    out_refs[0][...] = ckv_n
    out_refs[1][...] = kr_r
    if flash_layout:
        _, _, qt_ref, kc_ref, vt_ref = out_refs
        q_lat_t = q_lat.T
        q_rope_t = q_rope.T
        for h in range(MLA_HEADS):
            cols = slice(h * tm, (h + 1) * tm)
            qt_ref[0, 0:KV_LORA, cols] = q_lat_t[h * KV_LORA:(h + 1) * KV_LORA].astype(BF16)
            qt_ref[0, KV_LORA:QK_DIM, cols] = q_rope_t[h * QK_ROPE:(h + 1) * QK_ROPE].astype(BF16)
        kc_ref[...] = jnp.concatenate([ckv_n, kr_r], axis=1).astype(BF16)
        vt_ref[0, 0:KV_LORA, :] = ckv_n.T.astype(BF16)
        vt_ref[0, KV_LORA:VT_ROWS, :] = jnp.ones((VT_ROWS - KV_LORA, tm), BF16)
    else:
        _, _, ql_ref, qr_ref = out_refs
        ql_ref[...] = q_lat.astype(BF16)
        qr_ref[...] = q_rope.astype(BF16)


def _mla_prep(p_mla, q_g, w_uq, w_uk_bd, kv_g, cos, sin, rot, tm, flash_batch):
    n = p_mla.shape[0]
    nt = n // tm
    tab_tiles = cos.shape[0] // tm
    tab = pl.BlockSpec((tm, MLA_HEADS * QK_ROPE), lambda i: (i % tab_tiles, 0))
    out_shape = [jax.ShapeDtypeStruct((n, KV_LORA), F32), jax.ShapeDtypeStruct((n, QK_ROPE), F32)]
    out_specs = [_rows(tm, KV_LORA), _rows(tm, QK_ROPE)]
    if flash_batch is not None:
        tpb = nt // flash_batch
        out_shape += [jax.ShapeDtypeStruct((nt, QK_DIM, MLA_HEADS * tm), BF16),
                      jax.ShapeDtypeStruct((n, QK_DIM), BF16),
                      jax.ShapeDtypeStruct((flash_batch, VT_ROWS, n // flash_batch), BF16)]
        out_specs += [pl.BlockSpec((1, QK_DIM, MLA_HEADS * tm), lambda i: (i, 0, 0)),
                      _rows(tm, QK_DIM),
                      pl.BlockSpec((1, VT_ROWS, tm), lambda i: (i // tpb, 0, i % tpb))]
    else:
        out_shape += [jax.ShapeDtypeStruct((n, MLA_HEADS * KV_LORA), BF16),
                      jax.ShapeDtypeStruct((n, MLA_HEADS * QK_ROPE), BF16)]
        out_specs += [_rows(tm, MLA_HEADS * KV_LORA), _rows(tm, MLA_HEADS * QK_ROPE)]
    return pl.pallas_call(
        functools.partial(_mla_prep_kernel, flash_layout=flash_batch is not None),
        out_shape=tuple(out_shape),
        grid=(nt,),
        in_specs=[_rows(tm, MLA_PAD), _full(q_g), _full(w_uq), _full(w_uk_bd), _full(kv_g), tab, tab, _full(rot)],
        out_specs=tuple(out_specs),
        compiler_params=_cparams(("arbitrary",)),
        name="mla_prep",
    )(p_mla, q_g, w_uq, w_uk_bd, kv_g, cos, sin, rot)


FLASH_TK = 512
VT_ROWS = KV_LORA + 16


def _flash_kernel(qi_ref, kj_ref, qT_ref, k_ref, vT_ref, o_ref, m_scr, acc_scr, *, tq, tk):
    step = pl.program_id(1)
    i = qi_ref[step]
    j = kj_ref[step]
    rows = tq * MLA_HEADS
    q_lo = i * tq
    k_lo = j * tk
    last_j = (q_lo + tq - 1) // tk

    @pl.when(j == 0)
    def _():
        m_scr[...] = jnp.full((1, rows), -jnp.inf, F32)
        acc_scr[...] = jnp.zeros((VT_ROWS, rows), F32)

    def update(masked):
        s = _dg(k_ref[0], qT_ref[0], _NN)
        if masked:
            k_pos = k_lo + lax.broadcasted_iota(jnp.int32, s.shape, 0)
            q_pos = q_lo + lax.broadcasted_iota(jnp.int32, s.shape, 1) % tq
            s = jnp.where(k_pos <= q_pos, s, -jnp.inf)
        m_old = m_scr[...]
        m_new = jnp.maximum(m_old, jnp.max(s, axis=0, keepdims=True))
        pe = jnp.exp2(s - m_new).astype(BF16)
        acc_scr[...] = acc_scr[...] * jnp.exp2(m_old - m_new) + _dg(vT_ref[0], pe, _NN)
        m_scr[...] = m_new

    @pl.when(j < last_j)
    def _():
        update(False)

    @pl.when(j == last_j)
    def _():
        update(True)
        o = acc_scr[0:KV_LORA, :] / acc_scr[KV_LORA:KV_LORA + 1, :]
        for h in range(MLA_HEADS):
            o_ref[:, h * KV_LORA:(h + 1) * KV_LORA] = o[:, h * tq:(h + 1) * tq].T.astype(o_ref.dtype)


def _flash_attention(q_t, k, v_t, tq, tk):
    b, t, _ = k.shape
    rows = tq * MLA_HEADS
    nq = t // tq
    qi = np.concatenate([np.full(((i * tq + tq - 1) // tk + 1,), i, np.int32) for i in range(nq)])
    kj = np.concatenate([np.arange((i * tq + tq - 1) // tk + 1, dtype=np.int32) for i in range(nq)])
    grid_spec = pltpu.PrefetchScalarGridSpec(
        num_scalar_prefetch=2,
        grid=(b, len(qi)),
        in_specs=[pl.BlockSpec((1, QK_DIM, rows), lambda bb, s, qi, kj: (bb * nq + qi[s], 0, 0)),
                  pl.BlockSpec((1, tk, QK_DIM), lambda bb, s, qi, kj: (bb, kj[s], 0)),
                  pl.BlockSpec((1, VT_ROWS, tk), lambda bb, s, qi, kj: (bb, 0, kj[s]))],
        out_specs=pl.BlockSpec((tq, MLA_HEADS * KV_LORA), lambda bb, s, qi, kj: (bb * nq + qi[s], 0)),
        scratch_shapes=[pltpu.VMEM((1, rows), F32), pltpu.VMEM((VT_ROWS, rows), F32)],
    )
    return pl.pallas_call(
        functools.partial(_flash_kernel, tq=tq, tk=tk),
        out_shape=jax.ShapeDtypeStruct((b * t, MLA_HEADS * KV_LORA), BF16),
        grid_spec=grid_spec,
        compiler_params=_cparams(("arbitrary", "arbitrary")),
        name="mla_flash",
    )(jnp.asarray(qi), jnp.asarray(kj), q_t, k, v_t)


PAGES_PER_CHUNK = 32


def _paged_kernel(pt_ref, q_ref, kn_ref, ckv_hbm, krt_hbm, o_ref, ckv_buf, kr_buf, sem, *, layer, n_pages, t_new):
    b = pl.program_id(0)
    nb = pl.num_programs(0)
    pc = ckv_buf.shape[1]
    n_chunks = n_pages // pc
    rows = q_ref.shape[1]

    def copies(seq, chunk, slot, p):
        page = pt_ref[seq, chunk * pc + p]
        return (pltpu.make_async_copy(ckv_hbm.at[layer, page], ckv_buf.at[slot, p], sem.at[0, slot]),
                pltpu.make_async_copy(krt_hbm.at[layer, page], kr_buf.at[slot, p], sem.at[1, slot]))

    def start(seq, chunk, slot):
        for p in range(pc):
            for cp in copies(seq, chunk, slot, p):
                cp.start()

    def wait(seq, chunk, slot):
        for p in range(pc):
            for cp in copies(seq, chunk, slot, p):
                cp.wait()

    q = q_ref[0]
    q_lat = q[:, 0:KV_LORA]
    q_rope = q[:, KV_LORA:QK_DIM]

    def online(carry, s, vals):
        m_old, l_old, acc = carry
        m_new = jnp.maximum(m_old, jnp.max(s, axis=-1, keepdims=True))
        pe = jnp.exp2(s - m_new)
        corr = jnp.exp2(m_old - m_new)
        return (m_new, l_old * corr + jnp.sum(pe, axis=-1, keepdims=True),
                acc * corr + _dg(pe.astype(BF16), vals, _NN))

    @pl.when(b == 0)
    def _():
        start(0, 0, 0)

    def body(chunk, carry):
        g = b * n_chunks + chunk
        slot = g % 2

        @pl.when(chunk + 1 < n_chunks)
        def _():
            start(b, chunk + 1, 1 - slot)

        @pl.when((chunk + 1 == n_chunks) & (b + 1 < nb))
        def _():
            start(b + 1, 0, 1 - slot)

        wait(b, chunk, slot)
        kc = ckv_buf[slot].reshape(pc * PAGE_SIZE, KV_LORA).astype(BF16)
        s_rope = jnp.concatenate([_dg(q_rope, kr_buf[slot, p].astype(BF16), _NN) for p in range(pc)], axis=1)
        s = _dg(q_lat, kc, _NT) + s_rope
        return online(carry, s, kc)

    init = (jnp.full((rows, 1), -jnp.inf, F32), jnp.zeros((rows, 1), F32), jnp.zeros((rows, KV_LORA), F32))
    carry = lax.fori_loop(0, n_chunks, body, init)

    kn = kn_ref[0]
    s = _dg(q, kn, _NT)
    q_t = lax.broadcasted_iota(jnp.int32, s.shape, 0) // MLA_HEADS
    k_t = lax.broadcasted_iota(jnp.int32, s.shape, 1)
    s = jnp.where((k_t <= q_t) & (k_t < t_new), s, -jnp.inf)
    _, l_sum, acc = online(carry, s, kn[:, 0:KV_LORA])
    o_ref[0] = (acc / l_sum).astype(o_ref.dtype)


def _paged_attention(page_table, q, k_new, cache_ckv, cache_kr_t, layer, t_new):
    b, rows, _ = q.shape
    n_pages = page_table.shape[1]
    pc = math.gcd(PAGES_PER_CHUNK, n_pages)
    kn_rows = k_new.shape[1]
    grid_spec = pltpu.PrefetchScalarGridSpec(
        num_scalar_prefetch=1,
        grid=(b,),
        in_specs=[pl.BlockSpec((1, rows, QK_DIM), lambda i, pt: (i, 0, 0)),
                  pl.BlockSpec((1, kn_rows, QK_DIM), lambda i, pt: (i, 0, 0)),
                  pl.BlockSpec(memory_space=pl.ANY),
                  pl.BlockSpec(memory_space=pl.ANY)],
        out_specs=pl.BlockSpec((1, rows, KV_LORA), lambda i, pt: (i, 0, 0)),
        scratch_shapes=[pltpu.VMEM((2, pc, PAGE_SIZE, KV_LORA), F32),
                        pltpu.VMEM((2, pc, QK_ROPE, PAGE_SIZE), F32),
                        pltpu.SemaphoreType.DMA((2, 2))],
    )
    return pl.pallas_call(
        functools.partial(_paged_kernel, layer=layer, n_pages=n_pages, t_new=t_new),
        out_shape=jax.ShapeDtypeStruct((b, rows, KV_LORA), BF16),
        grid_spec=grid_spec,
        compiler_params=_cparams(("arbitrary",)),
        name="mla_paged",
    )(page_table, q, k_new, cache_ckv, cache_kr_t)


def _merge_kernel(x_ref, y_ref, r_ref, k_ref, v_ref, g_ref, ol_ref, gate_ref, gt_ref, ng_ref,
                  gnw_ref, gnb_ref, rk_ref, avg_ref, ones_ref, wo_ref, wuv_ref, mwo_ref, wout_ref, o_ref):
    y = y_ref[...]
    mean = _dot_exact_rhs(y, avg_ref[...])
    yc = y - mean
    var = _dot_exact_rhs(yc * yc, avg_ref[...])
    yn = yc * lax.rsqrt(var + GN_EPS) * gnw_ref[...] + gnb_ref[...]
    v = v_ref[...]
    bonus = _dot_exact_rhs(r_ref[...] * k_ref[...] * rk_ref[...], ones_ref[...]) * v
    o_rw = _dot1((yn + bonus) * g_ref[...], wo_ref[...])
    o_mla = _dot1(_dg(ol_ref[...], wuv_ref[...], _NN), mwo_ref[...])
    gate = _sigmoid(gate_ref[...])
    mix = _dot1(gate[:, 0:D_MODEL] * o_rw + gate[:, D_MODEL:] * o_mla, wout_ref[...])
    o_ref[...] = x_ref[...] + gt_ref[0] * _rms(mix, ng_ref[...])


def _merge(x, y, r, k, v, g, o_lat, p_gate, gt, ng, consts, tm, tiles_per_group):
    n = x.shape[0]
    return pl.pallas_call(
        _merge_kernel,
        out_shape=jax.ShapeDtypeStruct((n, D_MODEL), F32),
        grid=(n // tm,),
        in_specs=[_rows(tm, D_MODEL)] + [_rows(tm, RW_DIM)] * 5
                 + [_rows(tm, MLA_HEADS * KV_LORA), _rows(tm, 2 * D_MODEL), _mod_spec(gt, tm, tiles_per_group),
                    _full(ng)] + [_full(c) for c in consts],
        out_specs=_rows(tm, D_MODEL),
        compiler_params=_cparams(("arbitrary",)),
        name="merge",
    )(x, y, r, k, v, g, o_lat, p_gate, gt, ng, *consts)


FFN_CHUNK = 256


def _ffn_kernel(x_ref, g2_ref, sc_ref, sh_ref, gt_ref, g3_ref, win_ref, wout_ref, o_ref, *, d_ff):
    x = x_ref[...]
    hb = (_rms(x, g2_ref[...]) * (1.0 + sc_ref[0]) + sh_ref[0]).astype(BF16)
    acc = jnp.zeros(x.shape, F32)
    for c0 in range(0, d_ff, FFN_CHUNK):
        u = _dg(hb, win_ref[:, c0:c0 + FFN_CHUNK], _NN)
        vv = _dg(hb, win_ref[:, d_ff + c0:d_ff + c0 + FFN_CHUNK], _NN)
        act = (u * _sigmoid(u) * vv).astype(BF16)
        acc = acc + _dg(act, wout_ref[c0:c0 + FFN_CHUNK, :], _NN)
    o_ref[...] = x + gt_ref[0] * _rms(acc, g3_ref[...])


def _ffn(x, g2, sc, sh, gt, g3, w_in, w_out, tm, tiles_per_group):
    n = x.shape[0]
    d_ff = w_out.shape[0]
    ms = lambda m: _mod_spec(m, tm, tiles_per_group)
    return pl.pallas_call(
        functools.partial(_ffn_kernel, d_ff=d_ff),
        out_shape=jax.ShapeDtypeStruct((n, D_MODEL), F32),
        grid=(n // tm,),
        in_specs=[_rows(tm, D_MODEL), _full(g2), ms(sc), ms(sh), ms(gt), _full(g3), _full(w_in), _full(w_out)],
        out_specs=_rows(tm, D_MODEL),
        compiler_params=_cparams(("arbitrary",)),
        name="ffn",
    )(x, g2, sc, sh, gt, g3, w_in, w_out)


def _head_block_matrix(n_heads, width, value):
    idx = np.arange(n_heads * width) // width
    return jnp.asarray(np.where(idx[:, None] == idx[None, :], value, 0.0), dtype=BF16)


def _rotate_half_matrix(n_heads, width):
    half = width // 2
    m = np.zeros((n_heads * width, n_heads * width), np.float32)
    for h in range(n_heads):
        for i in range(half):
            m[h * width + half + i, h * width + i] = -1.0
            m[h * width + i, h * width + half + i] = 1.0
    return jnp.asarray(m, dtype=BF16)


def _rope_tables(pos, reps):
    half = QK_ROPE // 2
    inv = ROPE_THETA ** (-jnp.arange(half, dtype=F32) / half)
    ang = pos.astype(F32)[:, None] * inv[None, :]
    cos = jnp.tile(jnp.cos(ang), (1, 2 * reps))
    sin = jnp.tile(jnp.sin(ang), (1, 2 * reps))
    return cos, sin


def _block_diag_heads(w):
    h, a, b = w.shape
    eye = jnp.eye(h, dtype=w.dtype)
    return (eye[:, None, :, None] * w[:, :, None, :]).reshape(h * a, h * b)


def _state_to_pairs(s):
    b = s.shape[0]
    st = jnp.swapaxes(s, -1, -2).reshape(b, N_PAIRS, 2, RW_HEAD, RW_HEAD)
    eye = jnp.eye(2, dtype=s.dtype)
    return (st[:, :, :, :, None, :] * eye[None, None, :, None, :, None]).reshape(b, N_PAIRS, LANES, LANES)


def _pairs_to_state(h):
    b = h.shape[0]
    h6 = h.reshape(b, N_PAIRS, 2, RW_HEAD, 2, RW_HEAD)
    st = jnp.stack([h6[:, :, 0, :, 0, :], h6[:, :, 1, :, 1, :]], axis=2)
    return jnp.swapaxes(st, -1, -2).reshape(b, RW_HEADS, RW_HEAD, RW_HEAD)


def _prep_weights(P, layer):
    w = {}
    w_in = P['w_in'][layer]
    w['w_rw'] = w_in[:, :RW_COLS].astype(BF16)
    w['w_mla'] = jnp.pad(w_in[:, RW_COLS:RW_COLS + MLA_COLS], ((0, 0), (0, MLA_PAD - MLA_COLS))).astype(BF16)
    w['w_gate'] = w_in[:, RW_COLS + MLA_COLS:].astype(BF16)
    row = lambda z: z.reshape(1, -1)
    w['mu'] = row(P['rw_mu'][layer])
    w['wup'] = jnp.pad(P['rw_w_up'][layer], ((0, LORA_A), (0, 0))).astype(BF16)
    w['aup'] = jnp.pad(P['rw_a_up'][layer], ((LORA_W, 0), (0, 0))).astype(BF16)
    w['gup'] = P['rw_g_up'][layer].astype(BF16)
    for name in ('rw_w0', 'rw_a0', 'rw_k_k', 'rw_k_a', 'rw_gn_w', 'rw_gn_b', 'mla_q_g', 'mla_kv_g'):
        w[name] = row(P[name][layer])
    w['rw_r_k'] = P['rw_r_k'][layer].reshape(1, RW_DIM)
    w['rw_w_o'] = P['rw_w_o'][layer].astype(BF16)
    uq = P['mla_w_uq'][layer].reshape(Q_LORA, MLA_HEADS, QK_NOPE + QK_ROPE)
    w['w_uq'] = jnp.concatenate([uq[:, :, :QK_NOPE].reshape(Q_LORA, -1), uq[:, :, QK_NOPE:].reshape(Q_LORA, -1)],
                                axis=1).astype(BF16)
    w['w_uk'] = _block_diag_heads(jnp.transpose(P['mla_w_uk'][layer], (1, 2, 0))).astype(BF16)
    w['w_uv'] = _block_diag_heads(jnp.transpose(P['mla_w_uv'][layer], (1, 0, 2))).astype(BF16)
    w['mla_w_o'] = P['mla_w_o'][layer].astype(BF16)
    w['w_out'] = P['w_out'][layer].astype(BF16)
    w['ffn_w_in'] = P['ffn_w_in'][layer].astype(BF16)
    w['ffn_w_out'] = P['ffn_w_out'][layer].astype(BF16)
    w['norm_g'] = [row(P['norm_g'][layer][i]) for i in range(4)]
    return w


def _layer(layer, x, c, pos, shift0, wkv0, P, W, consts, attend):
    b, t, _ = x.shape
    n = b * t
    tm = min(256, n)
    xf = x.reshape(n, D_MODEL)
    mod = _mod_matmul(c, P['mod_w'], P['mod_b'], layer)
    if t % tm == 0:
        per_seq = lambda z: z.reshape(b, 1, z.shape[-1])
        tiles_per_group = t // tm
    else:
        per_seq = lambda z: jnp.repeat(z, t, axis=0).reshape(n // tm, tm, z.shape[-1])
        tiles_per_group = 1
    sh1, sc1, gt1, sh2, sc2, gt2 = [per_seq(z) for z in jnp.split(mod, 6, axis=-1)]
    ng = W['norm_g']

    p_rw, p_gate, p_mla = _in_proj(xf, ng[0], sc1, sh1, W['w_rw'], W['w_gate'], W['w_mla'], tm, tiles_per_group)

    shift_new = p_rw.reshape(b, t, RW_COLS)[:, -1]
    r, lw, k, v, kk, a, g = _rw_prep(p_rw, per_seq(shift0), t, W['mu'], W['wup'], W['rw_w0'], W['aup'], W['rw_a0'],
                                     W['gup'], W['rw_k_k'], W['rw_k_a'], consts['ones_blk'], tm, tiles_per_group)
    chunk = 64 if t % 64 == 0 else 8
    t_pad = -(-t // chunk) * chunk
    scan_in = [z.reshape(b, t, RW_DIM) for z in (r, lw, k, v, kk, a)]
    if t_pad != t:
        scan_in = [jnp.pad(z, ((0, 0), (0, t_pad - t), (0, 0))) for z in scan_in]
    y, h_t = _rw_scan(*scan_in, _state_to_pairs(wkv0), chunk, 2 if b % 2 == 0 else 1)
    y = y[:, :t].reshape(n, RW_DIM)
    wkv_new = _pairs_to_state(h_t)

    cos, sin = consts['rope'](pos, b, t, tm)
    mla_args = (p_mla, W['mla_q_g'], W['w_uq'], W['w_uk'], W['mla_kv_g'], cos, sin, consts['rot'], tm)
    if attend is None:
        ckv, kr, q_t, k_cat, v_t = _mla_prep(*mla_args, b)
        o_lat = _flash_attention(q_t, k_cat.reshape(b, t, QK_DIM), v_t, tm, min(FLASH_TK, t))
    else:
        ckv, kr, q_lat, q_rope = _mla_prep(*mla_args, None)
        q_cat = jnp.concatenate([q_lat.reshape(n, MLA_HEADS, KV_LORA), q_rope.reshape(n, MLA_HEADS, QK_ROPE)],
                                axis=-1).reshape(b, t * MLA_HEADS, QK_DIM)
        k_cat = jnp.concatenate([ckv, kr], axis=-1).astype(BF16).reshape(b, t, QK_DIM)
        o_lat = attend(layer, q_cat, k_cat).reshape(n, MLA_HEADS * KV_LORA)

    merge_consts = (W['rw_gn_w'], W['rw_gn_b'], W['rw_r_k'], consts['avg_blk'], consts['ones_blk'],
                    W['rw_w_o'], W['w_uv'], W['mla_w_o'], W['w_out'])
    x1 = _merge(xf, y, r, k, v, g, o_lat, p_gate, gt1, ng[1], merge_consts, tm, tiles_per_group)
    x2 = _ffn(x1, ng[2], sc2, sh2, gt2, ng[3], W['ffn_w_in'], W['ffn_w_out'], tm, tiles_per_group)
    return (x2.reshape(b, t, D_MODEL), ckv.reshape(b, t, KV_LORA), kr.reshape(b, t, QK_ROPE), wkv_new, shift_new)


def _trunk(x, c, pos, shift0s, wkv0s, P, Ws, consts, attend):
    depth = shift0s.shape[0]
    ckvs, krs, wkvs, shifts = [], [], [], []
    for layer in range(depth):
        x, ckv, kr, wkv, sh = _layer(layer, x, c, pos, shift0s[layer], wkv0s[layer], P, Ws[layer], consts, attend)
        ckvs.append(ckv)
        krs.append(kr)
        wkvs.append(wkv)
        shifts.append(sh)
    return x, jnp.stack(ckvs), jnp.stack(krs), jnp.stack(wkvs), jnp.stack(shifts)


def kernel(x_prompt, x_sample, cache_ckv, cache_kr, state_wkv, state_shift, page_table, c_prompt, c_sample, mod_w, mod_b, norm_g, w_in, rw_mu, rw_w_up, rw_w0, rw_a_up, rw_a0, rw_g_up, rw_k_k, rw_k_a, rw_r_k, rw_gn_w, rw_gn_b, rw_w_o, mla_q_g, mla_w_uq, mla_kv_g, mla_w_uk, mla_w_uv, mla_w_o, w_out, ffn_w_in, ffn_w_out):
    P = dict(mod_w=mod_w, mod_b=mod_b, norm_g=norm_g, w_in=w_in, rw_mu=rw_mu, rw_w_up=rw_w_up,
             rw_w0=rw_w0, rw_a_up=rw_a_up, rw_a0=rw_a0, rw_g_up=rw_g_up, rw_k_k=rw_k_k,
             rw_k_a=rw_k_a, rw_r_k=rw_r_k, rw_gn_w=rw_gn_w, rw_gn_b=rw_gn_b, rw_w_o=rw_w_o,
             mla_q_g=mla_q_g, mla_w_uq=mla_w_uq, mla_kv_g=mla_kv_g, mla_w_uk=mla_w_uk,
             mla_w_uv=mla_w_uv, mla_w_o=mla_w_o, w_out=w_out, ffn_w_in=ffn_w_in,
             ffn_w_out=ffn_w_out)
    depth = w_in.shape[0]
    Ws = [_prep_weights(P, layer) for layer in range(depth)]

    def rope(pos, b, t, tm):
        cos, sin = _rope_tables(pos, MLA_HEADS)
        if t % tm != 0:
            cos, sin = jnp.tile(cos, (b, 1)), jnp.tile(sin, (b, 1))
        return cos, sin

    consts = dict(ones_blk=_head_block_matrix(RW_HEADS, RW_HEAD, 1.0),
                  avg_blk=_head_block_matrix(RW_HEADS, RW_HEAD, 1.0 / RW_HEAD),
                  rot=_rotate_half_matrix(MLA_HEADS, QK_ROPE),
                  rope=rope)

    bp, tp, _ = x_prompt.shape
    zero_shift = jnp.zeros((depth, bp, RW_COLS), x_prompt.dtype)
    zero_wkv = jnp.zeros((depth, bp, RW_HEADS, RW_HEAD, RW_HEAD), state_wkv.dtype)

    y_prompt, ckv_p, kr_p, wkv_p, shift_p = _trunk(
        x_prompt, c_prompt, jnp.arange(tp), zero_shift, zero_wkv, P, Ws, consts, None)

    bs, ts, _ = x_sample.shape
    past = page_table.shape[1] * PAGE_SIZE
    cache_kr_t = jnp.swapaxes(cache_kr, 2, 3)

    def attend_sample(layer, q_cat, k_cat):
        k_new = jnp.pad(k_cat, ((0, 0), (0, 8 - ts), (0, 0)))
        return _paged_attention(page_table, q_cat, k_new, cache_ckv, cache_kr_t, layer, ts)

    y_sample, ckv_s, kr_s, wkv_s, shift_s = _trunk(
        x_sample, c_sample, past + jnp.arange(ts), state_shift, state_wkv, P, Ws, consts, attend_sample)
    return (y_prompt, y_sample, ckv_p, kr_p, wkv_p, shift_p, ckv_s, kr_s, wkv_s, shift_s)
```

```python
import functools
import math

import jax
import jax.numpy as jnp
import numpy as np
from jax import lax
from jax.experimental import pallas as pl
from jax.experimental.pallas import tpu as pltpu

F32 = jnp.float32
BF16 = jnp.bfloat16

D_MODEL = 1024
RW_HEADS = 8
RW_HEAD = 64
RW_DIM = RW_HEADS * RW_HEAD
LORA_W = 64
LORA_A = 64
LORA_G = 128
RW_COLS = 3 * RW_DIM + LORA_W + LORA_A + LORA_G
MLA_HEADS = 8
QK_NOPE = 64
QK_ROPE = 32
V_HEAD = 64
Q_LORA = 384
KV_LORA = 128
MLA_COLS = Q_LORA + KV_LORA + QK_ROPE
MLA_PAD = 640
QK_DIM = KV_LORA + QK_ROPE
PAGE_SIZE = 128

ROPE_THETA = 10000.0
NORM_EPS = 1e-6
GN_EPS = 64e-5
DECAY_SCALE = math.exp(-0.5)
ATTN_SCALE = (QK_NOPE + QK_ROPE) ** -0.5
LOG2_E = math.log2(math.e)

LANES = 128
N_PAIRS = RW_HEADS // 2
VMEM_LIMIT = 56 * 1024 * 1024


def _cparams(sem):
    return pltpu.CompilerParams(dimension_semantics=sem, vmem_limit_bytes=VMEM_LIMIT)


def _dg(a, b, dims):
    return lax.dot_general(a, b, (dims, ((), ())), preferred_element_type=F32)


_NN = ((1,), (0,))
_NT = ((1,), (1,))


def _dot1(a, b, dims=_NN):
    return _dg(a.astype(BF16), b.astype(BF16), dims)


def _split2(a):
    hi = a.astype(BF16)
    lo = (a - hi.astype(F32)).astype(BF16)
    return hi, lo


def _split3(a):
    hi = a.astype(BF16)
    r = a - hi.astype(F32)
    mid = r.astype(BF16)
    lo = (r - mid.astype(F32)).astype(BF16)
    return hi, mid, lo


def _dot3(a, b, dims=_NN):
    a1, a2 = _split2(a)
    b1, b2 = _split2(b)
    return _dg(a1, b1, dims) + (_dg(a1, b2, dims) + _dg(a2, b1, dims))


def _dot_exact_rhs(a, b_bf16, terms=3):
    if terms == 2:
        a1, a2 = _split2(a)
        return _dg(a1, b_bf16, _NN) + _dg(a2, b_bf16, _NN)
    a1, a2, a3 = _split3(a)
    return _dg(a1, b_bf16, _NN) + (_dg(a2, b_bf16, _NN) + _dg(a3, b_bf16, _NN))


def _dot_exact_lhs(a_bf16, b):
    b1, b2, b3 = _split3(b)
    return _dg(a_bf16, b1, _NN) + (_dg(a_bf16, b2, _NN) + _dg(a_bf16, b3, _NN))


def _rms(x, g):
    return x * lax.rsqrt(jnp.mean(x * x, axis=-1, keepdims=True) + NORM_EPS) * g


def _sigmoid(x):
    return 1.0 / (1.0 + jnp.exp(-x))


def _mod_kernel(c_ref, w_ref, b_ref, o_ref):
    o_ref[...] = _dot3(c_ref[...], w_ref[0]) + b_ref[0]


def _mod_matmul(c, w, b, layer):
    m, k = c.shape
    depth, _, n = w.shape
    tn = 1024
    return pl.pallas_call(
        _mod_kernel,
        out_shape=jax.ShapeDtypeStruct((m, n), F32),
        grid=(n // tn,),
        in_specs=[pl.BlockSpec((m, k), lambda j: (0, 0)),
                  pl.BlockSpec((1, k, tn), lambda j: (layer, 0, j)),
                  pl.BlockSpec((1, 1, tn), lambda j: (layer, 0, j))],
        out_specs=pl.BlockSpec((m, tn), lambda j: (0, j)),
        compiler_params=_cparams(("arbitrary",)),
        name="mod_matmul",
    )(c, w, b.reshape(depth, 1, n))


def _mod_spec(mod, tm, tiles_per_group):
    g, r, d = mod.shape
    if r == 1:
        return pl.BlockSpec((1, 1, d), lambda i: (i // tiles_per_group, 0, 0))
    return pl.BlockSpec((1, r, d), lambda i: (i, 0, 0))


def _full(a):
    nd = a.ndim
    return pl.BlockSpec(a.shape, lambda i: (0,) * nd)


def _rows(tm, width):
    return pl.BlockSpec((tm, width), lambda i: (i, 0))


def _in_proj_kernel(x_ref, g_ref, sc_ref, sh_ref, wrw_ref, wgt_ref, wml_ref, orw_ref, ogt_ref, oml_ref):
    h = _rms(x_ref[...], g_ref[...]) * (1.0 + sc_ref[0]) + sh_ref[0]
    hb = h.astype(BF16)
    orw_ref[...] = _dg(hb, wrw_ref[...], _NN)
    ogt_ref[...] = _sigmoid(_dg(hb, wgt_ref[...], _NN)).astype(BF16)
    oml_ref[...] = _dg(hb, wml_ref[...], _NN)


def _in_proj(x, g, sc, sh, w_rw, w_gate, w_mla, tm, tiles_per_group):
    n = x.shape[0]
    return pl.pallas_call(
        _in_proj_kernel,
        out_shape=(jax.ShapeDtypeStruct((n, RW_COLS), F32),
                   jax.ShapeDtypeStruct((n, 2 * D_MODEL), BF16),
                   jax.ShapeDtypeStruct((n, MLA_PAD), F32)),
        grid=(n // tm,),
        in_specs=[_rows(tm, D_MODEL), _full(g), _mod_spec(sc, tm, tiles_per_group),
                  _mod_spec(sh, tm, tiles_per_group), _full(w_rw), _full(w_gate), _full(w_mla)],
        out_specs=(_rows(tm, RW_COLS), _rows(tm, 2 * D_MODEL), _rows(tm, MLA_PAD)),
        compiler_params=_cparams(("arbitrary",)),
        name="in_proj",
    )(x, g, sc, sh, w_rw, w_gate, w_mla)


def _rw_prep_kernel(p_ref, s0_ref, mu_ref, wup_ref, w0_ref, aup_ref, a0_ref, gup_ref, kk_ref, ka_ref,
                    ones_ref, r_ref, lw_ref, k_ref, v_ref, kkn_ref, a_ref, g_ref, carry_scr, *, seq_len):
    i = pl.program_id(0)
    p = p_ref[...]
    tm = p.shape[0]

    @pl.when(i == 0)
    def _():
        carry_scr[...] = jnp.zeros(carry_scr.shape, F32)

    row = lax.broadcasted_iota(jnp.int32, (tm, 1), 0)
    above = jnp.where(row == 0, carry_scr[...], pltpu.roll(p, 1, 0))
    prev = jnp.where((i * tm + row) % seq_len == 0, s0_ref[0], above)
    carry_scr[...] = p[tm - 1:tm, :]
    m = p + (prev - p) * mu_ref[...]
    r = m[:, 0:RW_DIM]
    k = m[:, RW_DIM:2 * RW_DIM]
    v = m[:, 2 * RW_DIM:3 * RW_DIM]
    xwa = m[:, 3 * RW_DIM:3 * RW_DIM + LORA_W + LORA_A]
    xg = m[:, 3 * RW_DIM + LORA_W + LORA_A:RW_COLS]
    lw = -DECAY_SCALE * _sigmoid(w0_ref[...] + _dot1(jnp.tanh(xwa), wup_ref[...]))
    a = _sigmoid(a0_ref[...] + _dot1(xwa, aup_ref[...]))
    g = _dot1(_sigmoid(xg), gup_ref[...])
    kk = k * kk_ref[...]
    ss = _dot_exact_rhs(kk * kk, ones_ref[...], terms=2)
    kk = kk / jnp.maximum(jnp.sqrt(ss), 1e-12)
    r_ref[...] = r
    lw_ref[...] = lw
    k_ref[...] = k * (1.0 + (a - 1.0) * ka_ref[...])
    v_ref[...] = v
    kkn_ref[...] = kk
    a_ref[...] = a
    g_ref[...] = g


def _rw_prep(p_rw, shift0, seq_len, mu, wup, w0, aup, a0, gup, k_k, k_a, ones_blk, tm, tiles_per_group):
    n = p_rw.shape[0]
    out = jax.ShapeDtypeStruct((n, RW_DIM), F32)
    consts = (mu, wup, w0, aup, a0, gup, k_k, k_a, ones_blk)
    return pl.pallas_call(
        functools.partial(_rw_prep_kernel, seq_len=seq_len),
        out_shape=(out,) * 7,
        grid=(n // tm,),
        in_specs=[_rows(tm, RW_COLS), _mod_spec(shift0, tm, tiles_per_group)] + [_full(c) for c in consts],
        out_specs=(_rows(tm, RW_DIM),) * 7,
        scratch_shapes=[pltpu.VMEM((1, RW_COLS), F32)],
        compiler_params=_cparams(("arbitrary",)),
        name="rw_prep",
    )(p_rw, shift0, *consts)


def _rw_scan_kernel(r_ref, lw_ref, k_ref, v_ref, kk_ref, a_ref, h0_ref, y_ref, hT_ref, h_scr, *, chunk, bb):
    c = pl.program_id(1)
    nc = pl.num_programs(1)
    C = chunk

    @pl.when(c == 0)
    def _():
        h_scr[...] = h0_ref[...]

    ti = lax.broadcasted_iota(jnp.int32, (C, C), 0)
    si = lax.broadcasted_iota(jnp.int32, (C, C), 1)
    strict = ti > si
    incl = ti >= si
    ltri = jnp.where(incl, 1.0, 0.0).astype(BF16)
    eye_c = jnp.where(ti == si, 1.0, 0.0).astype(F32)
    lane = lax.broadcasted_iota(jnp.int32, (1, LANES), 1)
    head_mask = [jnp.where(lane < RW_HEAD, 1.0, 0.0).astype(F32), jnp.where(lane >= RW_HEAD, 1.0, 0.0).astype(F32)]
    rj = lax.broadcasted_iota(jnp.int32, (LANES, LANES), 0)
    cj = lax.broadcasted_iota(jnp.int32, (LANES, LANES), 1)
    blockdiag = (rj < RW_HEAD) == (cj < RW_HEAD)
    eye_l = rj == cj
    levels = []
    s = 1
    while s < C:
        levels.append((ti // (2 * s) == si // (2 * s)) & ((ti // s) % 2 == 1) & ((si // s) % 2 == 0))
        s *= 2
    zeros_cl = jnp.zeros((C, LANES), F32)

    pairs = [(bi, p) for bi in range(bb) for p in range(N_PAIRS)]
    heads = [(q, hh) for q in range(len(pairs)) for hh in range(2)]
    load = lambda ref: [ref[bi, :, p * LANES:(p + 1) * LANES] for bi, p in pairs]
    r, lw, k, v, kk, a = (load(ref) for ref in (r_ref, lw_ref, k_ref, v_ref, kk_ref, a_ref))
    H = [h_scr[bi, p] for bi, p in pairs]

    cum = [_dot_exact_lhs(ltri, x) for x in lw]
    cum_last = [x[C - 1:C, :] for x in cum]
    inv_g = [jnp.exp(-x) for x in cum]
    at = [-kk[q] * jnp.exp(cum[q] - lw[q]) for q in range(len(pairs))]
    rt = [r[q] * jnp.exp(cum[q]) for q in range(len(pairs))]
    beta = [kk[q] * a[q] for q in range(len(pairs))]
    bh = [beta[q] * inv_g[q] for q in range(len(pairs))]
    kh = [k[q] * inv_g[q] for q in range(len(pairs))]
    to_end = [jnp.exp(cum_last[q] - cum[q]) for q in range(len(pairs))]
    x_ar = [jnp.concatenate([at[q], rt[q]], axis=0) for q in range(len(pairs))]

    g_b = [_dot1(x_ar[q], bh[q] * head_mask[hh], _NT) for q, hh in heads]
    g_k = [_dot1(x_ar[q], kh[q] * head_mask[hh], _NT) for q, hh in heads]
    a_ab = [jnp.where(strict, g[:C], 0.0) for g in g_b]
    a_rb = [jnp.where(incl, g[C:], 0.0) for g in g_b]
    a_kk = [jnp.concatenate([jnp.where(strict, g[:C], 0.0), jnp.where(incl, g[C:], 0.0)], axis=0) for g in g_k]
    x12 = [_dot1(a_kk[i], v[q]) for i, (q, hh) in enumerate(heads)]
    T = [eye_c + jnp.where(levels[0], x, 0.0) for x in a_ab]
    for lm in levels[1:]:
        ta = [_dot1(T[i], jnp.where(lm, a_ab[i], 0.0)) for i in range(len(heads))]
        T = [T[i] + _dot1(ta[i], T[i]) for i in range(len(heads))]
    tu = [_dot1(T[i], jnp.concatenate([at[q], x12[i][:C]], axis=1)) for i, (q, hh) in enumerate(heads)]
    ry = [_dot1(a_rb[i], tu[i]) for i in range(len(heads))]

    def combine(f):
        return [head_mask[0] * f(2 * q) + head_mask[1] * f(2 * q + 1) for q in range(len(pairs))]

    at_p = combine(lambda i: tu[i][:, :LANES])
    u0_p = combine(lambda i: tu[i][:, LANES:])
    rp_p = combine(lambda i: ry[i][:, :LANES])
    y0_p = combine(lambda i: ry[i][:, LANES:] + x12[i][C:])

    y = [y0_p[q] + _dot3(rt[q] + rp_p[q], H[q]) for q in range(len(pairs))]
    lhs_t = [jnp.concatenate([beta[q] * to_end[q], k[q] * to_end[q]], axis=0).T for q in range(len(pairs))]
    rhs = [jnp.concatenate([jnp.concatenate([at_p[q], u0_p[q]], axis=1),
                            jnp.concatenate([zeros_cl, v[q]], axis=1)], axis=0) for q in range(len(pairs))]
    pq = [_dot1(lhs_t[q], rhs[q]) for q in range(len(pairs))]
    P = [jnp.where(blockdiag, pq[q][:, :LANES], 0.0)
         + jnp.where(eye_l, jnp.broadcast_to(jnp.exp(cum_last[q]), (LANES, LANES)), 0.0) for q in range(len(pairs))]
    h_new = [_dot3(P[q], H[q]) + jnp.where(blockdiag, pq[q][:, LANES:], 0.0) for q in range(len(pairs))]

    for q, (bi, p) in enumerate(pairs):
        y_ref[bi, :, p * LANES:(p + 1) * LANES] = y[q]
        h_scr[bi, p] = h_new[q]

    @pl.when(c == nc - 1)
    def _():
        hT_ref[...] = h_scr[...]


def _rw_scan(r, lw, k, v, kk, a, h0, chunk, bb):
    b, t, _ = r.shape
    nc = t // chunk
    row = pl.BlockSpec((bb, chunk, RW_DIM), lambda g, c: (g, c, 0))
    hspec = pl.BlockSpec((bb, N_PAIRS, LANES, LANES), lambda g, c: (g, 0, 0, 0))
    return pl.pallas_call(
        functools.partial(_rw_scan_kernel, chunk=chunk, bb=bb),
        out_shape=(jax.ShapeDtypeStruct((b, t, RW_DIM), F32),
                   jax.ShapeDtypeStruct((b, N_PAIRS, LANES, LANES), F32)),
        grid=(b // bb, nc),
        in_specs=[row] * 6 + [hspec],
        out_specs=(row, hspec),
        scratch_shapes=[pltpu.VMEM((bb, N_PAIRS, LANES, LANES), F32)],
        compiler_params=_cparams(("arbitrary", "arbitrary")),
        name="rw_scan",
    )(r, lw, k, v, kk, a, h0)


def _mla_prep_kernel(p_ref, qg_ref, wuq_ref, wuk_ref, kvg_ref, cos_ref, sin_ref, rot_ref, *out_refs, flash_layout):
    p = p_ref[...]
    tm = p.shape[0]
    cq = p[:, 0:Q_LORA]
    ckv = p[:, Q_LORA:Q_LORA + KV_LORA]
    kr = p[:, Q_LORA + KV_LORA:MLA_COLS]
    cos = cos_ref[...]
    sin = sin_ref[...]
    rot = rot_ref[...]
    q = _dot1(_rms(cq, qg_ref[...]), wuq_ref[...])
    q_nope = q[:, 0:MLA_HEADS * QK_NOPE]
    q_rope = q[:, MLA_HEADS * QK_NOPE:]
    q_rope = q_rope * cos + _dot_exact_rhs(q_rope, rot, terms=2) * sin
    q_scale = ATTN_SCALE * LOG2_E
    q_lat = _dot1(q_nope, wuk_ref[...]) * q_scale
    q_rope = q_rope * q_scale
    ckv_n = _rms(ckv, kvg_ref[...])
    kr_r = kr * cos[:, 0:QK_ROPE] + _dot_exact_rhs(kr, rot[0:QK_ROPE, 0:QK_ROPE]) * sin[:, 0:QK_ROPE]
    out_refs[0][...] = ckv_n
    out_refs[1][...] = kr_r
    if flash_layout:
        _, _, qt_ref, kc_ref, vt_ref = out_refs
        q_lat_t = q_lat.T
        q_rope_t = q_rope.T
        for h in range(MLA_HEADS):
            cols = slice(h * tm, (h + 1) * tm)
            qt_ref[0, 0:KV_LORA, cols] = q_lat_t[h * KV_LORA:(h + 1) * KV_LORA].astype(BF16)
            qt_ref[0, KV_LORA:QK_DIM, cols] = q_rope_t[h * QK_ROPE:(h + 1) * QK_ROPE].astype(BF16)
        kc_ref[...] = jnp.concatenate([ckv_n, kr_r], axis=1).astype(BF16)
        vt_ref[0, 0:KV_LORA, :] = ckv_n.T.astype(BF16)
        vt_ref[0, KV_LORA:VT_ROWS, :] = jnp.ones((VT_ROWS - KV_LORA, tm), BF16)
    else:
        _, _, ql_ref, qr_ref = out_refs
        ql_ref[...] = q_lat.astype(BF16)
        qr_ref[...] = q_rope.astype(BF16)


def _mla_prep(p_mla, q_g, w_uq, w_uk_bd, kv_g, cos, sin, rot, tm, flash_batch):
    n = p_mla.shape[0]
    nt = n // tm
    tab_tiles = cos.shape[0] // tm
    tab = pl.BlockSpec((tm, MLA_HEADS * QK_ROPE), lambda i: (i % tab_tiles, 0))
    out_shape = [jax.ShapeDtypeStruct((n, KV_LORA), F32), jax.ShapeDtypeStruct((n, QK_ROPE), F32)]
    out_specs = [_rows(tm, KV_LORA), _rows(tm, QK_ROPE)]
    if flash_batch is not None:
        tpb = nt // flash_batch
        out_shape += [jax.ShapeDtypeStruct((nt, QK_DIM, MLA_HEADS * tm), BF16),
                      jax.ShapeDtypeStruct((n, QK_DIM), BF16),
                      jax.ShapeDtypeStruct((flash_batch, VT_ROWS, n // flash_batch), BF16)]
        out_specs += [pl.BlockSpec((1, QK_DIM, MLA_HEADS * tm), lambda i: (i, 0, 0)),
                      _rows(tm, QK_DIM),
                      pl.BlockSpec((1, VT_ROWS, tm), lambda i: (i // tpb, 0, i % tpb))]
    else:
        out_shape += [jax.ShapeDtypeStruct((n, MLA_HEADS * KV_LORA), BF16),
                      jax.ShapeDtypeStruct((n, MLA_HEADS * QK_ROPE), BF16)]
        out_specs += [_rows(tm, MLA_HEADS * KV_LORA), _rows(tm, MLA_HEADS * QK_ROPE)]
    return pl.pallas_call(
        functools.partial(_mla_prep_kernel, flash_layout=flash_batch is not None),
        out_shape=tuple(out_shape),
        grid=(nt,),
        in_specs=[_rows(tm, MLA_PAD), _full(q_g), _full(w_uq), _full(w_uk_bd), _full(kv_g), tab, tab, _full(rot)],
        out_specs=tuple(out_specs),
        compiler_params=_cparams(("arbitrary",)),
        name="mla_prep",
    )(p_mla, q_g, w_uq, w_uk_bd, kv_g, cos, sin, rot)


FLASH_TK = 512
VT_ROWS = KV_LORA + 16


def _flash_kernel(qi_ref, kj_ref, qT_ref, k_ref, vT_ref, o_ref, m_scr, acc_scr, *, tq, tk):
    step = pl.program_id(1)
    i = qi_ref[step]
    j = kj_ref[step]
    rows = tq * MLA_HEADS
    q_lo = i * tq
    k_lo = j * tk
    last_j = (q_lo + tq - 1) // tk

    @pl.when(j == 0)
    def _():
        m_scr[...] = jnp.full((1, rows), -jnp.inf, F32)
        acc_scr[...] = jnp.zeros((VT_ROWS, rows), F32)

    def update(masked):
        s = _dg(k_ref[0], qT_ref[0], _NN)
        if masked:
            k_pos = k_lo + lax.broadcasted_iota(jnp.int32, s.shape, 0)
            q_pos = q_lo + lax.broadcasted_iota(jnp.int32, s.shape, 1) % tq
            s = jnp.where(k_pos <= q_pos, s, -jnp.inf)
        m_old = m_scr[...]
        m_new = jnp.maximum(m_old, jnp.max(s, axis=0, keepdims=True))
        pe = jnp.exp2(s - m_new).astype(BF16)
        acc_scr[...] = acc_scr[...] * jnp.exp2(m_old - m_new) + _dg(vT_ref[0], pe, _NN)
        m_scr[...] = m_new

    @pl.when(j < last_j)
    def _():
        update(False)

    @pl.when(j == last_j)
    def _():
        update(True)
        o = acc_scr[0:KV_LORA, :] / acc_scr[KV_LORA:KV_LORA + 1, :]
        for h in range(MLA_HEADS):
            o_ref[:, h * KV_LORA:(h + 1) * KV_LORA] = o[:, h * tq:(h + 1) * tq].T.astype(o_ref.dtype)


def _flash_attention(q_t, k, v_t, tq, tk):
    b, t, _ = k.shape
    rows = tq * MLA_HEADS
    nq = t // tq
    qi = np.concatenate([np.full(((i * tq + tq - 1) // tk + 1,), i, np.int32) for i in range(nq)])
    kj = np.concatenate([np.arange((i * tq + tq - 1) // tk + 1, dtype=np.int32) for i in range(nq)])
    grid_spec = pltpu.PrefetchScalarGridSpec(
        num_scalar_prefetch=2,
        grid=(b, len(qi)),
        in_specs=[pl.BlockSpec((1, QK_DIM, rows), lambda bb, s, qi, kj: (bb * nq + qi[s], 0, 0)),
                  pl.BlockSpec((1, tk, QK_DIM), lambda bb, s, qi, kj: (bb, kj[s], 0)),
                  pl.BlockSpec((1, VT_ROWS, tk), lambda bb, s, qi, kj: (bb, 0, kj[s]))],
        out_specs=pl.BlockSpec((tq, MLA_HEADS * KV_LORA), lambda bb, s, qi, kj: (bb * nq + qi[s], 0)),
        scratch_shapes=[pltpu.VMEM((1, rows), F32), pltpu.VMEM((VT_ROWS, rows), F32)],
    )
    return pl.pallas_call(
        functools.partial(_flash_kernel, tq=tq, tk=tk),
        out_shape=jax.ShapeDtypeStruct((b * t, MLA_HEADS * KV_LORA), BF16),
        grid_spec=grid_spec,
        compiler_params=_cparams(("arbitrary", "arbitrary")),
        name="mla_flash",
    )(jnp.asarray(qi), jnp.asarray(kj), q_t, k, v_t)


PAGES_PER_CHUNK = 32


def _paged_kernel(pt_ref, q_ref, kn_ref, ckv_hbm, krt_hbm, o_ref, ckv_buf, kr_buf, sem, *, layer, n_pages, t_new):
    b = pl.program_id(0)
    nb = pl.num_programs(0)
    pc = ckv_buf.shape[1]
    n_chunks = n_pages // pc
    rows = q_ref.shape[1]

    def copies(seq, chunk, slot, p):
        page = pt_ref[seq, chunk * pc + p]
        return (pltpu.make_async_copy(ckv_hbm.at[layer, page], ckv_buf.at[slot, p], sem.at[0, slot]),
                pltpu.make_async_copy(krt_hbm.at[layer, page], kr_buf.at[slot, p], sem.at[1, slot]))

    def start(seq, chunk, slot):
        for p in range(pc):
            for cp in copies(seq, chunk, slot, p):
                cp.start()

    def wait(seq, chunk, slot):
        for p in range(pc):
            for cp in copies(seq, chunk, slot, p):
                cp.wait()

    q = q_ref[0]
    q_lat = q[:, 0:KV_LORA]
    q_rope = q[:, KV_LORA:QK_DIM]

    def block_stats(s, vals):
        m = jnp.max(s, axis=-1, keepdims=True)
        pe = jnp.exp2(s - m)
        return m, jnp.sum(pe, axis=-1, keepdims=True), _dg(pe.astype(BF16), vals, _NN)

    def merge(x, y):
        m = jnp.maximum(x[0], y[0])
        cx = jnp.exp2(x[0] - m)
        cy = jnp.exp2(y[0] - m)
        return m, x[1] * cx + y[1] * cy, x[2] * cx + y[2] * cy

    @pl.when(b == 0)
    def _():
        start(0, 0, 0)

    first_slot = 0 if n_chunks % 2 == 0 else (b * n_chunks) % 2
    stats = []
    for chunk in range(n_chunks):
        slot = (first_slot + chunk) % 2
        if chunk + 1 < n_chunks:
            start(b, chunk + 1, 1 - slot)
        else:
            @pl.when(b + 1 < nb)
            def _():
                start(b + 1, 0, 1 - slot)

        wait(b, chunk, slot)
        kc = ckv_buf[slot].reshape(pc * PAGE_SIZE, KV_LORA).astype(BF16)
        s_rope = jnp.concatenate([_dg(q_rope, kr_buf[slot, p].astype(BF16), _NN) for p in range(pc)], axis=1)
        stats.append(block_stats(_dg(q_lat, kc, _NT) + s_rope, kc))

    kn = kn_ref[0]
    s = _dg(q, kn, _NT)
    q_t = lax.broadcasted_iota(jnp.int32, s.shape, 0) // MLA_HEADS
    k_t = lax.broadcasted_iota(jnp.int32, s.shape, 1)
    s = jnp.where((k_t <= q_t) & (k_t < t_new), s, -jnp.inf)
    stats.append(block_stats(s, kn[:, 0:KV_LORA]))
    while len(stats) > 1:
        stats = [merge(stats[i], stats[i + 1]) if i + 1 < len(stats) else stats[i] for i in range(0, len(stats), 2)]
    _, l_sum, acc = stats[0]
    o_ref[0] = (acc / l_sum).astype(o_ref.dtype)


def _paged_attention(page_table, q, k_new, cache_ckv, cache_kr_t, layer, t_new):
    b, rows, _ = q.shape
    n_pages = page_table.shape[1]
    pc = math.gcd(PAGES_PER_CHUNK, n_pages)
    kn_rows = k_new.shape[1]
    grid_spec = pltpu.PrefetchScalarGridSpec(
        num_scalar_prefetch=1,
        grid=(b,),
        in_specs=[pl.BlockSpec((1, rows, QK_DIM), lambda i, pt: (i, 0, 0)),
                  pl.BlockSpec((1, kn_rows, QK_DIM), lambda i, pt: (i, 0, 0)),
                  pl.BlockSpec(memory_space=pl.ANY),
                  pl.BlockSpec(memory_space=pl.ANY)],
        out_specs=pl.BlockSpec((1, rows, KV_LORA), lambda i, pt: (i, 0, 0)),
        scratch_shapes=[pltpu.VMEM((2, pc, PAGE_SIZE, KV_LORA), F32),
                        pltpu.VMEM((2, pc, QK_ROPE, PAGE_SIZE), F32),
                        pltpu.SemaphoreType.DMA((2, 2))],
    )
    return pl.pallas_call(
        functools.partial(_paged_kernel, layer=layer, n_pages=n_pages, t_new=t_new),
        out_shape=jax.ShapeDtypeStruct((b, rows, KV_LORA), BF16),
        grid_spec=grid_spec,
        compiler_params=_cparams(("arbitrary",)),
        name="mla_paged",
    )(page_table, q, k_new, cache_ckv, cache_kr_t)


def _merge_kernel(x_ref, y_ref, r_ref, k_ref, v_ref, g_ref, ol_ref, gate_ref, gt_ref, ng_ref,
                  gnw_ref, gnb_ref, rk_ref, avg_ref, ones_ref, wo_ref, wuv_ref, mwo_ref, wout_ref, o_ref):
    y = y_ref[...]
    mean = _dot_exact_rhs(y, avg_ref[...], terms=2)
    yc = y - mean
    var = _dot_exact_rhs(yc * yc, avg_ref[...], terms=2)
    yn = yc * lax.rsqrt(var + GN_EPS) * gnw_ref[...] + gnb_ref[...]
    v = v_ref[...]
    bonus = _dot_exact_rhs(r_ref[...] * k_ref[...] * rk_ref[...], ones_ref[...], terms=2) * v
    o_rw = _dot1((yn + bonus) * g_ref[...], wo_ref[...])
    o_mla = _dot1(_dg(ol_ref[...], wuv_ref[...], _NN), mwo_ref[...])
    gate = gate_ref[...].astype(F32)
    mix = _dot1(gate[:, 0:D_MODEL] * o_rw + gate[:, D_MODEL:] * o_mla, wout_ref[...])
    o_ref[...] = x_ref[...] + gt_ref[0] * _rms(mix, ng_ref[...])


def _merge(x, y, r, k, v, g, o_lat, p_gate, gt, ng, consts, tm, tiles_per_group):
    n = x.shape[0]
    return pl.pallas_call(
        _merge_kernel,
        out_shape=jax.ShapeDtypeStruct((n, D_MODEL), F32),
        grid=(n // tm,),
        in_specs=[_rows(tm, D_MODEL)] + [_rows(tm, RW_DIM)] * 5
                 + [_rows(tm, MLA_HEADS * KV_LORA), _rows(tm, 2 * D_MODEL), _mod_spec(gt, tm, tiles_per_group),
                    _full(ng)] + [_full(c) for c in consts],
        out_specs=_rows(tm, D_MODEL),
        compiler_params=_cparams(("arbitrary",)),
        name="merge",
    )(x, y, r, k, v, g, o_lat, p_gate, gt, ng, *consts)


FFN_CHUNK = 256


def _ffn_kernel(x_ref, g2_ref, sc_ref, sh_ref, gt_ref, g3_ref, win_ref, wout_ref, o_ref, *, d_ff):
    x = x_ref[...]
    hb = (_rms(x, g2_ref[...]) * (1.0 + sc_ref[0]) + sh_ref[0]).astype(BF16)
    acc = jnp.zeros(x.shape, F32)
    for c0 in range(0, d_ff, FFN_CHUNK):
        u = _dg(hb, win_ref[:, c0:c0 + FFN_CHUNK], _NN)
        vv = _dg(hb, win_ref[:, d_ff + c0:d_ff + c0 + FFN_CHUNK], _NN)
        act = (u * _sigmoid(u) * vv).astype(BF16)
        acc = acc + _dg(act, wout_ref[c0:c0 + FFN_CHUNK, :], _NN)
    o_ref[...] = x + gt_ref[0] * _rms(acc, g3_ref[...])


def _ffn(x, g2, sc, sh, gt, g3, w_in, w_out, tm, tiles_per_group):
    n = x.shape[0]
    d_ff = w_out.shape[0]
    ms = lambda m: _mod_spec(m, tm, tiles_per_group)
    return pl.pallas_call(
        functools.partial(_ffn_kernel, d_ff=d_ff),
        out_shape=jax.ShapeDtypeStruct((n, D_MODEL), F32),
        grid=(n // tm,),
        in_specs=[_rows(tm, D_MODEL), _full(g2), ms(sc), ms(sh), ms(gt), _full(g3), _full(w_in), _full(w_out)],
        out_specs=_rows(tm, D_MODEL),
        compiler_params=_cparams(("arbitrary",)),
        name="ffn",
    )(x, g2, sc, sh, gt, g3, w_in, w_out)


def _head_block_matrix(n_heads, width, value):
    idx = np.arange(n_heads * width) // width
    return jnp.asarray(np.where(idx[:, None] == idx[None, :], value, 0.0), dtype=BF16)


def _rotate_half_matrix(n_heads, width):
    half = width // 2
    m = np.zeros((n_heads * width, n_heads * width), np.float32)
    for h in range(n_heads):
        for i in range(half):
            m[h * width + half + i, h * width + i] = -1.0
            m[h * width + i, h * width + half + i] = 1.0
    return jnp.asarray(m, dtype=BF16)


def _rope_tables(pos, reps):
    half = QK_ROPE // 2
    inv = ROPE_THETA ** (-jnp.arange(half, dtype=F32) / half)
    ang = pos.astype(F32)[:, None] * inv[None, :]
    cos = jnp.tile(jnp.cos(ang), (1, 2 * reps))
    sin = jnp.tile(jnp.sin(ang), (1, 2 * reps))
    return cos, sin


def _block_diag_heads(w):
    h, a, b = w.shape
    eye = jnp.eye(h, dtype=w.dtype)
    return (eye[:, None, :, None] * w[:, :, None, :]).reshape(h * a, h * b)


def _state_to_pairs(s):
    b = s.shape[0]
    st = jnp.swapaxes(s, -1, -2).reshape(b, N_PAIRS, 2, RW_HEAD, RW_HEAD)
    eye = jnp.eye(2, dtype=s.dtype)
    return (st[:, :, :, :, None, :] * eye[None, None, :, None, :, None]).reshape(b, N_PAIRS, LANES, LANES)


def _pairs_to_state(h):
    b = h.shape[0]
    h6 = h.reshape(b, N_PAIRS, 2, RW_HEAD, 2, RW_HEAD)
    st = jnp.stack([h6[:, :, 0, :, 0, :], h6[:, :, 1, :, 1, :]], axis=2)
    return jnp.swapaxes(st, -1, -2).reshape(b, RW_HEADS, RW_HEAD, RW_HEAD)


def _prep_weights(P, layer):
    w = {}
    w_in = P['w_in'][layer]
    w['w_rw'] = w_in[:, :RW_COLS].astype(BF16)
    w['w_mla'] = jnp.pad(w_in[:, RW_COLS:RW_COLS + MLA_COLS], ((0, 0), (0, MLA_PAD - MLA_COLS))).astype(BF16)
    w['w_gate'] = w_in[:, RW_COLS + MLA_COLS:].astype(BF16)
    row = lambda z: z.reshape(1, -1)
    w['mu'] = row(P['rw_mu'][layer])
    w['wup'] = jnp.pad(P['rw_w_up'][layer], ((0, LORA_A), (0, 0))).astype(BF16)
    w['aup'] = jnp.pad(P['rw_a_up'][layer], ((LORA_W, 0), (0, 0))).astype(BF16)
    w['gup'] = P['rw_g_up'][layer].astype(BF16)
    for name in ('rw_w0', 'rw_a0', 'rw_k_k', 'rw_k_a', 'rw_gn_w', 'rw_gn_b', 'mla_q_g', 'mla_kv_g'):
        w[name] = row(P[name][layer])
    w['rw_r_k'] = P['rw_r_k'][layer].reshape(1, RW_DIM)
    w['rw_w_o'] = P['rw_w_o'][layer].astype(BF16)
    uq = P['mla_w_uq'][layer].reshape(Q_LORA, MLA_HEADS, QK_NOPE + QK_ROPE)
    w['w_uq'] = jnp.concatenate([uq[:, :, :QK_NOPE].reshape(Q_LORA, -1), uq[:, :, QK_NOPE:].reshape(Q_LORA, -1)],
                                axis=1).astype(BF16)
    w['w_uk'] = _block_diag_heads(jnp.transpose(P['mla_w_uk'][layer], (1, 2, 0))).astype(BF16)
    w['w_uv'] = _block_diag_heads(jnp.transpose(P['mla_w_uv'][layer], (1, 0, 2))).astype(BF16)
    w['mla_w_o'] = P['mla_w_o'][layer].astype(BF16)
    w['w_out'] = P['w_out'][layer].astype(BF16)
    w['ffn_w_in'] = P['ffn_w_in'][layer].astype(BF16)
    w['ffn_w_out'] = P['ffn_w_out'][layer].astype(BF16)
    w['norm_g'] = [row(P['norm_g'][layer][i]) for i in range(4)]
    return w


def _layer(layer, x, c, pos, shift0, wkv0, P, W, consts, attend):
    b, t, _ = x.shape
    n = b * t
    tm = min(256, n)
    xf = x.reshape(n, D_MODEL)
    mod = _mod_matmul(c, P['mod_w'], P['mod_b'], layer)
    if t % tm == 0:
        per_seq = lambda z: z.reshape(b, 1, z.shape[-1])
        tiles_per_group = t // tm
    else:
        per_seq = lambda z: jnp.repeat(z, t, axis=0).reshape(n // tm, tm, z.shape[-1])
        tiles_per_group = 1
    sh1, sc1, gt1, sh2, sc2, gt2 = [per_seq(z) for z in jnp.split(mod, 6, axis=-1)]
    ng = W['norm_g']

    p_rw, p_gate, p_mla = _in_proj(xf, ng[0], sc1, sh1, W['w_rw'], W['w_gate'], W['w_mla'], tm, tiles_per_group)

    shift_new = p_rw.reshape(b, t, RW_COLS)[:, -1]
    r, lw, k, v, kk, a, g = _rw_prep(p_rw, per_seq(shift0), t, W['mu'], W['wup'], W['rw_w0'], W['aup'], W['rw_a0'],
                                     W['gup'], W['rw_k_k'], W['rw_k_a'], consts['ones_blk'], tm, tiles_per_group)
    chunk = 64 if t % 64 == 0 else 8
    t_pad = -(-t // chunk) * chunk
    scan_in = [z.reshape(b, t, RW_DIM) for z in (r, lw, k, v, kk, a)]
    if t_pad != t:
        scan_in = [jnp.pad(z, ((0, 0), (0, t_pad - t), (0, 0))) for z in scan_in]
    y, h_t = _rw_scan(*scan_in, _state_to_pairs(wkv0), chunk, 2 if b % 2 == 0 else 1)
    y = y[:, :t].reshape(n, RW_DIM)
    wkv_new = _pairs_to_state(h_t)

    cos, sin = consts['rope'](pos, b, t, tm)
    mla_args = (p_mla, W['mla_q_g'], W['w_uq'], W['w_uk'], W['mla_kv_g'], cos, sin, consts['rot'], tm)
    if attend is None:
        ckv, kr, q_t, k_cat, v_t = _mla_prep(*mla_args, b)
        o_lat = _flash_attention(q_t, k_cat.reshape(b, t, QK_DIM), v_t, tm, min(FLASH_TK, t))
    else:
        ckv, kr, q_lat, q_rope = _mla_prep(*mla_args, None)
        q_cat = jnp.concatenate([q_lat.reshape(n, MLA_HEADS, KV_LORA), q_rope.reshape(n, MLA_HEADS, QK_ROPE)],
                                axis=-1).reshape(b, t * MLA_HEADS, QK_DIM)
        k_cat = jnp.concatenate([ckv, kr], axis=-1).astype(BF16).reshape(b, t, QK_DIM)
        o_lat = attend(layer, q_cat, k_cat).reshape(n, MLA_HEADS * KV_LORA)

    merge_consts = (W['rw_gn_w'], W['rw_gn_b'], W['rw_r_k'], consts['avg_blk'], consts['ones_blk'],
                    W['rw_w_o'], W['w_uv'], W['mla_w_o'], W['w_out'])
    x1 = _merge(xf, y, r, k, v, g, o_lat, p_gate, gt1, ng[1], merge_consts, tm, tiles_per_group)
    x2 = _ffn(x1, ng[2], sc2, sh2, gt2, ng[3], W['ffn_w_in'], W['ffn_w_out'], tm, tiles_per_group)
    return (x2.reshape(b, t, D_MODEL), ckv.reshape(b, t, KV_LORA), kr.reshape(b, t, QK_ROPE), wkv_new, shift_new)


def _trunk(x, c, pos, shift0s, wkv0s, P, Ws, consts, attend):
    depth = shift0s.shape[0]
    ckvs, krs, wkvs, shifts = [], [], [], []
    for layer in range(depth):
        x, ckv, kr, wkv, sh = _layer(layer, x, c, pos, shift0s[layer], wkv0s[layer], P, Ws[layer], consts, attend)
        ckvs.append(ckv)
        krs.append(kr)
        wkvs.append(wkv)
        shifts.append(sh)
    return x, jnp.stack(ckvs), jnp.stack(krs), jnp.stack(wkvs), jnp.stack(shifts)


def kernel(x_prompt, x_sample, cache_ckv, cache_kr, state_wkv, state_shift, page_table, c_prompt, c_sample, mod_w, mod_b, norm_g, w_in, rw_mu, rw_w_up, rw_w0, rw_a_up, rw_a0, rw_g_up, rw_k_k, rw_k_a, rw_r_k, rw_gn_w, rw_gn_b, rw_w_o, mla_q_g, mla_w_uq, mla_kv_g, mla_w_uk, mla_w_uv, mla_w_o, w_out, ffn_w_in, ffn_w_out):
    P = dict(mod_w=mod_w, mod_b=mod_b, norm_g=norm_g, w_in=w_in, rw_mu=rw_mu, rw_w_up=rw_w_up,
             rw_w0=rw_w0, rw_a_up=rw_a_up, rw_a0=rw_a0, rw_g_up=rw_g_up, rw_k_k=rw_k_k,
             rw_k_a=rw_k_a, rw_r_k=rw_r_k, rw_gn_w=rw_gn_w, rw_gn_b=rw_gn_b, rw_w_o=rw_w_o,
             mla_q_g=mla_q_g, mla_w_uq=mla_w_uq, mla_kv_g=mla_kv_g, mla_w_uk=mla_w_uk,
             mla_w_uv=mla_w_uv, mla_w_o=mla_w_o, w_out=w_out, ffn_w_in=ffn_w_in,
             ffn_w_out=ffn_w_out)
    depth = w_in.shape[0]
    Ws = [_prep_weights(P, layer) for layer in range(depth)]

    def rope(pos, b, t, tm):
        cos, sin = _rope_tables(pos, MLA_HEADS)
        if t % tm != 0:
            cos, sin = jnp.tile(cos, (b, 1)), jnp.tile(sin, (b, 1))
        return cos, sin

    consts = dict(ones_blk=_head_block_matrix(RW_HEADS, RW_HEAD, 1.0),
                  avg_blk=_head_block_matrix(RW_HEADS, RW_HEAD, 1.0 / RW_HEAD),
                  rot=_rotate_half_matrix(MLA_HEADS, QK_ROPE),
                  rope=rope)

    bp, tp, _ = x_prompt.shape
    zero_shift = jnp.zeros((depth, bp, RW_COLS), x_prompt.dtype)
    zero_wkv = jnp.zeros((depth, bp, RW_HEADS, RW_HEAD, RW_HEAD), state_wkv.dtype)

    y_prompt, ckv_p, kr_p, wkv_p, shift_p = _trunk(
        x_prompt, c_prompt, jnp.arange(tp), zero_shift, zero_wkv, P, Ws, consts, None)

    bs, ts, _ = x_sample.shape
    past = page_table.shape[1] * PAGE_SIZE
    cache_kr_t = jnp.swapaxes(cache_kr, 2, 3)

    def attend_sample(layer, q_cat, k_cat):
        k_new = jnp.pad(k_cat, ((0, 0), (0, 8 - ts), (0, 0)))
        return _paged_attention(page_table, q_cat, k_new, cache_ckv, cache_kr_t, layer, ts)

    y_sample, ckv_s, kr_s, wkv_s, shift_s = _trunk(
        x_sample, c_sample, past + jnp.arange(ts), state_shift, state_wkv, P, Ws, consts, attend_sample)
    return (y_prompt, y_sample, ckv_p, kr_p, wkv_p, shift_p, ckv_s, kr_s, wkv_s, shift_s)
```

```python
import functools
import math

import jax
import jax.numpy as jnp
import numpy as np
from jax import lax
from jax.experimental import pallas as pl
from jax.experimental.pallas import tpu as pltpu

F32 = jnp.float32
BF16 = jnp.bfloat16

D_MODEL = 1024
RW_HEADS = 8
RW_HEAD = 64
RW_DIM = RW_HEADS * RW_HEAD
LORA_W = 64
LORA_A = 64
LORA_G = 128
RW_COLS = 3 * RW_DIM + LORA_W + LORA_A + LORA_G
MLA_HEADS = 8
QK_NOPE = 64
QK_ROPE = 32
V_HEAD = 64
Q_LORA = 384
KV_LORA = 128
MLA_COLS = Q_LORA + KV_LORA + QK_ROPE
MLA_PAD = 640
QK_DIM = KV_LORA + QK_ROPE
PAGE_SIZE = 128

ROPE_THETA = 10000.0
NORM_EPS = 1e-6
GN_EPS = 64e-5
DECAY_SCALE = math.exp(-0.5)
ATTN_SCALE = (QK_NOPE + QK_ROPE) ** -0.5
LOG2_E = math.log2(math.e)

LANES = 128
N_PAIRS = RW_HEADS // 2
VMEM_LIMIT = 56 * 1024 * 1024


def _cparams(sem):
    return pltpu.CompilerParams(dimension_semantics=sem, vmem_limit_bytes=VMEM_LIMIT)


def _dg(a, b, dims):
    return lax.dot_general(a, b, (dims, ((), ())), preferred_element_type=F32)


_NN = ((1,), (0,))
_NT = ((1,), (1,))


def _dot1(a, b, dims=_NN):
    return _dg(a.astype(BF16), b.astype(BF16), dims)


def _split2(a):
    hi = a.astype(BF16)
    lo = (a - hi.astype(F32)).astype(BF16)
    return hi, lo


def _split3(a):
    hi = a.astype(BF16)
    r = a - hi.astype(F32)
    mid = r.astype(BF16)
    lo = (r - mid.astype(F32)).astype(BF16)
    return hi, mid, lo


def _dot3(a, b, dims=_NN):
    a1, a2 = _split2(a)
    b1, b2 = _split2(b)
    return _dg(a1, b1, dims) + (_dg(a1, b2, dims) + _dg(a2, b1, dims))


def _dot_exact_rhs(a, b_bf16, terms=3):
    if terms == 2:
        a1, a2 = _split2(a)
        return _dg(a1, b_bf16, _NN) + _dg(a2, b_bf16, _NN)
    a1, a2, a3 = _split3(a)
    return _dg(a1, b_bf16, _NN) + (_dg(a2, b_bf16, _NN) + _dg(a3, b_bf16, _NN))


def _dot_exact_lhs(a_bf16, b):
    b1, b2, b3 = _split3(b)
    return _dg(a_bf16, b1, _NN) + (_dg(a_bf16, b2, _NN) + _dg(a_bf16, b3, _NN))


def _rms(x, g):
    return x * lax.rsqrt(jnp.mean(x * x, axis=-1, keepdims=True) + NORM_EPS) * g


def _sigmoid(x):
    return 1.0 / (1.0 + jnp.exp(-x))


def _mod_kernel(c_ref, w_ref, b_ref, o_ref):
    o_ref[...] = _dot3(c_ref[...], w_ref[0]) + b_ref[0]


def _mod_matmul(c, w, b, layer):
    m, k = c.shape
    depth, _, n = w.shape
    tn = 1024
    return pl.pallas_call(
        _mod_kernel,
        out_shape=jax.ShapeDtypeStruct((m, n), F32),
        grid=(n // tn,),
        in_specs=[pl.BlockSpec((m, k), lambda j: (0, 0)),
                  pl.BlockSpec((1, k, tn), lambda j: (layer, 0, j)),
                  pl.BlockSpec((1, 1, tn), lambda j: (layer, 0, j))],
        out_specs=pl.BlockSpec((m, tn), lambda j: (0, j)),
        compiler_params=_cparams(("arbitrary",)),
        name="mod_matmul",
    )(c, w, b.reshape(depth, 1, n))


def _mod_spec(mod, tm, tiles_per_group):
    g, r, d = mod.shape
    if r == 1:
        return pl.BlockSpec((1, 1, d), lambda i: (i // tiles_per_group, 0, 0))
    return pl.BlockSpec((1, r, d), lambda i: (i, 0, 0))


def _full(a):
    nd = a.ndim
    return pl.BlockSpec(a.shape, lambda i: (0,) * nd)


def _rows(tm, width):
    return pl.BlockSpec((tm, width), lambda i: (i, 0))


def _in_proj_kernel(x_ref, ng_ref, sc_ref, sh_ref, wrw_ref, wgt_ref, wml_ref,
                    s0_ref, mu_ref, wup_ref, w0_ref, aup_ref, a0_ref, gup_ref, kk_ref, ka_ref, ones_ref,
                    orw_ref, ogt_ref, oml_ref, r_ref, lw_ref, k_ref, v_ref, kkn_ref, a_ref, g_ref,
                    carry_scr, *, seq_len):
    i = pl.program_id(0)
    h = _rms(x_ref[...], ng_ref[...]) * (1.0 + sc_ref[0]) + sh_ref[0]
    hb = h.astype(BF16)
    p = _dg(hb, wrw_ref[...], _NN)
    orw_ref[...] = p
    ogt_ref[...] = _sigmoid(_dg(hb, wgt_ref[...], _NN)).astype(BF16)
    oml_ref[...] = _dg(hb, wml_ref[...], _NN)
    tm = p.shape[0]

    @pl.when(i == 0)
    def _():
        carry_scr[...] = jnp.zeros(carry_scr.shape, F32)

    row = lax.broadcasted_iota(jnp.int32, (tm, 1), 0)
    above = jnp.where(row == 0, carry_scr[...], pltpu.roll(p, 1, 0))
    prev = jnp.where((i * tm + row) % seq_len == 0, s0_ref[0], above)
    carry_scr[...] = p[tm - 1:tm, :]
    m = p + (prev - p) * mu_ref[...]
    r = m[:, 0:RW_DIM]
    k = m[:, RW_DIM:2 * RW_DIM]
    v = m[:, 2 * RW_DIM:3 * RW_DIM]
    xwa = m[:, 3 * RW_DIM:3 * RW_DIM + LORA_W + LORA_A]
    xg = m[:, 3 * RW_DIM + LORA_W + LORA_A:RW_COLS]
    lw = -DECAY_SCALE * _sigmoid(w0_ref[...] + _dot1(jnp.tanh(xwa), wup_ref[...]))
    a = _sigmoid(a0_ref[...] + _dot1(xwa, aup_ref[...]))
    g = _dot1(_sigmoid(xg), gup_ref[...])
    kk = k * kk_ref[...]
    ss = _dot_exact_rhs(kk * kk, ones_ref[...], terms=2)
    kk = kk / jnp.maximum(jnp.sqrt(ss), 1e-12)
    r_ref[...] = r
    lw_ref[...] = lw
    k_ref[...] = k * (1.0 + (a - 1.0) * ka_ref[...])
    v_ref[...] = v
    kkn_ref[...] = kk
    a_ref[...] = a
    g_ref[...] = g


def _in_proj(x, g, sc, sh, w_rw, w_gate, w_mla, shift0, seq_len, rw_consts, tm, tiles_per_group):
    n = x.shape[0]
    rw_out = jax.ShapeDtypeStruct((n, RW_DIM), F32)
    return pl.pallas_call(
        functools.partial(_in_proj_kernel, seq_len=seq_len),
        out_shape=(jax.ShapeDtypeStruct((n, RW_COLS), F32),
                   jax.ShapeDtypeStruct((n, 2 * D_MODEL), BF16),
                   jax.ShapeDtypeStruct((n, MLA_PAD), F32)) + (rw_out,) * 7,
        grid=(n // tm,),
        in_specs=[_rows(tm, D_MODEL), _full(g), _mod_spec(sc, tm, tiles_per_group),
                  _mod_spec(sh, tm, tiles_per_group), _full(w_rw), _full(w_gate), _full(w_mla),
                  _mod_spec(shift0, tm, tiles_per_group)] + [_full(c) for c in rw_consts],
        out_specs=(_rows(tm, RW_COLS), _rows(tm, 2 * D_MODEL), _rows(tm, MLA_PAD)) + (_rows(tm, RW_DIM),) * 7,
        scratch_shapes=[pltpu.VMEM((1, RW_COLS), F32)],
        compiler_params=_cparams(("arbitrary",)),
        name="in_proj",
    )(x, g, sc, sh, w_rw, w_gate, w_mla, shift0, *rw_consts)


def _rw_scan_kernel(r_ref, lw_ref, k_ref, v_ref, kk_ref, a_ref, h0_ref, y_ref, hT_ref, h_scr, *, chunk, bb):
    c = pl.program_id(1)
    nc = pl.num_programs(1)
    C = chunk

    @pl.when(c == 0)
    def _():
        h_scr[...] = h0_ref[...]

    ti = lax.broadcasted_iota(jnp.int32, (C, C), 0)
    si = lax.broadcasted_iota(jnp.int32, (C, C), 1)
    strict = ti > si
    incl = ti >= si
    ltri = jnp.where(incl, 1.0, 0.0).astype(BF16)
    eye_c = jnp.where(ti == si, 1.0, 0.0).astype(F32)
    lane = lax.broadcasted_iota(jnp.int32, (1, LANES), 1)
    head_mask = [jnp.where(lane < RW_HEAD, 1.0, 0.0).astype(F32), jnp.where(lane >= RW_HEAD, 1.0, 0.0).astype(F32)]
    rj = lax.broadcasted_iota(jnp.int32, (LANES, LANES), 0)
    cj = lax.broadcasted_iota(jnp.int32, (LANES, LANES), 1)
    blockdiag = (rj < RW_HEAD) == (cj < RW_HEAD)
    eye_l = rj == cj
    levels = []
    s = 1
    while s < C:
        levels.append((ti // (2 * s) == si // (2 * s)) & ((ti // s) % 2 == 1) & ((si // s) % 2 == 0))
        s *= 2
    zeros_cl = jnp.zeros((C, LANES), F32)

    pairs = [(bi, p) for bi in range(bb) for p in range(N_PAIRS)]
    heads = [(q, hh) for q in range(len(pairs)) for hh in range(2)]
    load = lambda ref: [ref[bi, :, p * LANES:(p + 1) * LANES] for bi, p in pairs]
    r, lw, k, v, kk, a = (load(ref) for ref in (r_ref, lw_ref, k_ref, v_ref, kk_ref, a_ref))
    H = [h_scr[bi, p] for bi, p in pairs]

    cum = [_dot_exact_lhs(ltri, x) for x in lw]
    cum_last = [x[C - 1:C, :] for x in cum]
    inv_g = [jnp.exp(-x) for x in cum]
    at = [-kk[q] * jnp.exp(cum[q] - lw[q]) for q in range(len(pairs))]
    rt = [r[q] * jnp.exp(cum[q]) for q in range(len(pairs))]
    beta = [kk[q] * a[q] for q in range(len(pairs))]
    bh = [beta[q] * inv_g[q] for q in range(len(pairs))]
    kh = [k[q] * inv_g[q] for q in range(len(pairs))]
    to_end = [jnp.exp(cum_last[q] - cum[q]) for q in range(len(pairs))]
    x_ar = [jnp.concatenate([at[q], rt[q]], axis=0) for q in range(len(pairs))]

    g_b = [_dot1(x_ar[q], bh[q] * head_mask[hh], _NT) for q, hh in heads]
    g_k = [_dot1(x_ar[q], kh[q] * head_mask[hh], _NT) for q, hh in heads]
    a_ab = [jnp.where(strict, g[:C], 0.0) for g in g_b]
    a_rb = [jnp.where(incl, g[C:], 0.0) for g in g_b]
    a_kk = [jnp.concatenate([jnp.where(strict, g[:C], 0.0), jnp.where(incl, g[C:], 0.0)], axis=0) for g in g_k]
    x12 = [_dot1(a_kk[i], v[q]) for i, (q, hh) in enumerate(heads)]
    T = [eye_c + jnp.where(levels[0], x, 0.0) for x in a_ab]
    for lm in levels[1:]:
        ta = [_dot1(T[i], jnp.where(lm, a_ab[i], 0.0)) for i in range(len(heads))]
        T = [T[i] + _dot1(ta[i], T[i]) for i in range(len(heads))]
    tu = [_dot1(T[i], jnp.concatenate([at[q], x12[i][:C]], axis=1)) for i, (q, hh) in enumerate(heads)]
    ry = [_dot1(a_rb[i], tu[i]) for i in range(len(heads))]

    def combine(f):
        return [head_mask[0] * f(2 * q) + head_mask[1] * f(2 * q + 1) for q in range(len(pairs))]

    at_p = combine(lambda i: tu[i][:, :LANES])
    u0_p = combine(lambda i: tu[i][:, LANES:])
    rp_p = combine(lambda i: ry[i][:, :LANES])
    y0_p = combine(lambda i: ry[i][:, LANES:] + x12[i][C:])

    y = [y0_p[q] + _dot3(rt[q] + rp_p[q], H[q]) for q in range(len(pairs))]
    lhs_t = [jnp.concatenate([beta[q] * to_end[q], k[q] * to_end[q]], axis=0).T for q in range(len(pairs))]
    rhs = [jnp.concatenate([jnp.concatenate([at_p[q], u0_p[q]], axis=1),
                            jnp.concatenate([zeros_cl, v[q]], axis=1)], axis=0) for q in range(len(pairs))]
    pq = [_dot1(lhs_t[q], rhs[q]) for q in range(len(pairs))]
    P = [jnp.where(blockdiag, pq[q][:, :LANES], 0.0)
         + jnp.where(eye_l, jnp.broadcast_to(jnp.exp(cum_last[q]), (LANES, LANES)), 0.0) for q in range(len(pairs))]
    h_new = [_dot3(P[q], H[q]) + jnp.where(blockdiag, pq[q][:, LANES:], 0.0) for q in range(len(pairs))]

    for q, (bi, p) in enumerate(pairs):
        y_ref[bi, :, p * LANES:(p + 1) * LANES] = y[q]
        h_scr[bi, p] = h_new[q]

    @pl.when(c == nc - 1)
    def _():
        hT_ref[...] = h_scr[...]


def _rw_scan(r, lw, k, v, kk, a, h0, chunk, bb):
    b, t, _ = r.shape
    nc = t // chunk
    row = pl.BlockSpec((bb, chunk, RW_DIM), lambda g, c: (g, c, 0))
    hspec = pl.BlockSpec((bb, N_PAIRS, LANES, LANES), lambda g, c: (g, 0, 0, 0))
    return pl.pallas_call(
        functools.partial(_rw_scan_kernel, chunk=chunk, bb=bb),
        out_shape=(jax.ShapeDtypeStruct((b, t, RW_DIM), F32),
                   jax.ShapeDtypeStruct((b, N_PAIRS, LANES, LANES), F32)),
        grid=(b // bb, nc),
        in_specs=[row] * 6 + [hspec],
        out_specs=(row, hspec),
        scratch_shapes=[pltpu.VMEM((bb, N_PAIRS, LANES, LANES), F32)],
        compiler_params=_cparams(("arbitrary", "arbitrary")),
        name="rw_scan",
    )(r, lw, k, v, kk, a, h0)


def _mla_prep_kernel(p_ref, qg_ref, wuq_ref, wuk_ref, kvg_ref, cos_ref, sin_ref, rot_ref, *out_refs, flash_layout):
    p = p_ref[...]
    tm = p.shape[0]
    cq = p[:, 0:Q_LORA]
    ckv = p[:, Q_LORA:Q_LORA + KV_LORA]
    kr = p[:, Q_LORA + KV_LORA:MLA_COLS]
    cos = cos_ref[...]
    sin = sin_ref[...]
    rot = rot_ref[...]
    q = _dot1(_rms(cq, qg_ref[...]), wuq_ref[...])
    q_nope = q[:, 0:MLA_HEADS * QK_NOPE]
    q_rope = q[:, MLA_HEADS * QK_NOPE:]
    q_rope = q_rope * cos + _dot_exact_rhs(q_rope, rot, terms=2) * sin
    q_scale = ATTN_SCALE * LOG2_E
    q_lat = _dot1(q_nope, wuk_ref[...]) * q_scale
    q_rope = q_rope * q_scale
    ckv_n = _rms(ckv, kvg_ref[...])
    kr_r = kr * cos[:, 0:QK_ROPE] + _dot_exact_rhs(kr, rot[0:QK_ROPE, 0:QK_ROPE]) * sin[:, 0:QK_ROPE]
    out_refs[0][...] = ckv_n
    out_refs[1][...] = kr_r
    if flash_layout:
        _, _, qt_ref, kc_ref, vt_ref = out_refs
        q_lat_t = q_lat.T
        q_rope_t = q_rope.T
        for h in range(MLA_HEADS):
            cols = slice(h * tm, (h + 1) * tm)
            qt_ref[0, 0:KV_LORA, cols] = q_lat_t[h * KV_LORA:(h + 1) * KV_LORA].astype(BF16)
            qt_ref[0, KV_LORA:QK_DIM, cols] = q_rope_t[h * QK_ROPE:(h + 1) * QK_ROPE].astype(BF16)
        kc_ref[...] = jnp.concatenate([ckv_n, kr_r], axis=1).astype(BF16)
        vt_ref[0, 0:KV_LORA, :] = ckv_n.T.astype(BF16)
        vt_ref[0, KV_LORA:VT_ROWS, :] = jnp.ones((VT_ROWS - KV_LORA, tm), BF16)
    else:
        _, _, ql_ref, qr_ref = out_refs
        ql_ref[...] = q_lat.astype(BF16)
        qr_ref[...] = q_rope.astype(BF16)


def _mla_prep(p_mla, q_g, w_uq, w_uk_bd, kv_g, cos, sin, rot, tm, flash_batch):
    n = p_mla.shape[0]
    nt = n // tm
    tab_tiles = cos.shape[0] // tm
    tab = pl.BlockSpec((tm, MLA_HEADS * QK_ROPE), lambda i: (i % tab_tiles, 0))
    out_shape = [jax.ShapeDtypeStruct((n, KV_LORA), F32), jax.ShapeDtypeStruct((n, QK_ROPE), F32)]
    out_specs = [_rows(tm, KV_LORA), _rows(tm, QK_ROPE)]
    if flash_batch is not None:
        tpb = nt // flash_batch
        out_shape += [jax.ShapeDtypeStruct((nt, QK_DIM, MLA_HEADS * tm), BF16),
                      jax.ShapeDtypeStruct((n, QK_DIM), BF16),
                      jax.ShapeDtypeStruct((flash_batch, VT_ROWS, n // flash_batch), BF16)]
        out_specs += [pl.BlockSpec((1, QK_DIM, MLA_HEADS * tm), lambda i: (i, 0, 0)),
                      _rows(tm, QK_DIM),
                      pl.BlockSpec((1, VT_ROWS, tm), lambda i: (i // tpb, 0, i % tpb))]
    else:
        out_shape += [jax.ShapeDtypeStruct((n, MLA_HEADS * KV_LORA), BF16),
                      jax.ShapeDtypeStruct((n, MLA_HEADS * QK_ROPE), BF16)]
        out_specs += [_rows(tm, MLA_HEADS * KV_LORA), _rows(tm, MLA_HEADS * QK_ROPE)]
    return pl.pallas_call(
        functools.partial(_mla_prep_kernel, flash_layout=flash_batch is not None),
        out_shape=tuple(out_shape),
        grid=(nt,),
        in_specs=[_rows(tm, MLA_PAD), _full(q_g), _full(w_uq), _full(w_uk_bd), _full(kv_g), tab, tab, _full(rot)],
        out_specs=tuple(out_specs),
        compiler_params=_cparams(("arbitrary",)),
        name="mla_prep",
    )(p_mla, q_g, w_uq, w_uk_bd, kv_g, cos, sin, rot)


FLASH_TQ = 512
FLASH_TK = 512
VT_ROWS = KV_LORA + 16


def _flash_kernel(qi_ref, kj_ref, qT_ref, k_ref, vT_ref, o_ref, m_scr, acc_scr, *, tq, tk):
    step = pl.program_id(1)
    i = qi_ref[step]
    j = kj_ref[step]
    rows = tq * MLA_HEADS
    q_lo = i * tq
    k_lo = j * tk
    last_j = (q_lo + tq - 1) // tk

    @pl.when(j == 0)
    def _():
        m_scr[...] = jnp.full((1, rows), -jnp.inf, F32)
        acc_scr[...] = jnp.zeros((VT_ROWS, rows), F32)

    def update(masked):
        s = _dg(k_ref[0], qT_ref[0], _NN)
        if masked:
            k_pos = k_lo + lax.broadcasted_iota(jnp.int32, s.shape, 0)
            q_pos = q_lo + lax.broadcasted_iota(jnp.int32, s.shape, 1) % tq
            s = jnp.where(k_pos <= q_pos, s, -jnp.inf)
        m_old = m_scr[...]
        m_new = jnp.maximum(m_old, jnp.max(s, axis=0, keepdims=True))
        pe = jnp.exp2(s - m_new).astype(BF16)
        acc_scr[...] = acc_scr[...] * jnp.exp2(m_old - m_new) + _dg(vT_ref[0], pe, _NN)
        m_scr[...] = m_new

    @pl.when(j < last_j)
    def _():
        update(False)

    @pl.when(j == last_j)
    def _():
        update(True)
        o = acc_scr[0:KV_LORA, :] / acc_scr[KV_LORA:KV_LORA + 1, :]
        for h in range(MLA_HEADS):
            o_ref[:, h * KV_LORA:(h + 1) * KV_LORA] = o[:, h * tq:(h + 1) * tq].T.astype(o_ref.dtype)


def _flash_attention(q_t, k, v_t, tq, tk):
    b, t, _ = k.shape
    rows = tq * MLA_HEADS
    nq = t // tq
    qi = np.concatenate([np.full(((i * tq + tq - 1) // tk + 1,), i, np.int32) for i in range(nq)])
    kj = np.concatenate([np.arange((i * tq + tq - 1) // tk + 1, dtype=np.int32) for i in range(nq)])
    grid_spec = pltpu.PrefetchScalarGridSpec(
        num_scalar_prefetch=2,
        grid=(b, len(qi)),
        in_specs=[pl.BlockSpec((1, QK_DIM, rows), lambda bb, s, qi, kj: (bb * nq + qi[s], 0, 0)),
                  pl.BlockSpec((1, tk, QK_DIM), lambda bb, s, qi, kj: (bb, kj[s], 0)),
                  pl.BlockSpec((1, VT_ROWS, tk), lambda bb, s, qi, kj: (bb, 0, kj[s]))],
        out_specs=pl.BlockSpec((tq, MLA_HEADS * KV_LORA), lambda bb, s, qi, kj: (bb * nq + qi[s], 0)),
        scratch_shapes=[pltpu.VMEM((1, rows), F32), pltpu.VMEM((VT_ROWS, rows), F32)],
    )
    return pl.pallas_call(
        functools.partial(_flash_kernel, tq=tq, tk=tk),
        out_shape=jax.ShapeDtypeStruct((b * t, MLA_HEADS * KV_LORA), BF16),
        grid_spec=grid_spec,
        compiler_params=_cparams(("arbitrary", "arbitrary")),
        name="mla_flash",
    )(jnp.asarray(qi), jnp.asarray(kj), q_t, k, v_t)


PAGES_PER_CHUNK = 32
PAGED_SLOTS = 3


def _paged_kernel(pt_ref, q_ref, kn_ref, ckv_hbm, krt_hbm, o_ref, ckv_buf, kr_buf, sem, *, layer, n_pages, t_new):
    b = pl.program_id(0)
    nb = pl.num_programs(0)
    pc = ckv_buf.shape[1]
    n_chunks = n_pages // pc
    rows = q_ref.shape[1]

    def copies(seq, chunk, slot, p):
        page = pt_ref[seq, chunk * pc + p]
        return (pltpu.make_async_copy(ckv_hbm.at[layer, page], ckv_buf.at[slot, p], sem.at[0, slot]),
                pltpu.make_async_copy(krt_hbm.at[layer, page], kr_buf.at[slot, p], sem.at[1, slot]))

    def start(seq, chunk, slot):
        for p in range(pc):
            for cp in copies(seq, chunk, slot, p):
                cp.start()

    def wait(seq, chunk, slot):
        for p in range(pc):
            for cp in copies(seq, chunk, slot, p):
                cp.wait()

    q = q_ref[0]
    q_lat = q[:, 0:KV_LORA]
    q_rope = q[:, KV_LORA:QK_DIM]

    def block_stats(s, vals):
        m = jnp.max(s, axis=-1, keepdims=True)
        pe = jnp.exp2(s - m)
        return m, jnp.sum(pe, axis=-1, keepdims=True), _dg(pe.astype(BF16), vals, _NN)

    def merge(x, y):
        m = jnp.maximum(x[0], y[0])
        cx = jnp.exp2(x[0] - m)
        cy = jnp.exp2(y[0] - m)
        return m, x[1] * cx + y[1] * cy, x[2] * cx + y[2] * cy

    n_slots = ckv_buf.shape[0]
    ahead = n_slots - 1
    first_slot = lax.rem(b * n_chunks, n_slots)

    @pl.when(b == 0)
    def _():
        for g in range(min(ahead, pt_ref.shape[0] * n_chunks)):
            start(g // n_chunks, g % n_chunks, g % n_slots)

    stats = []
    for chunk in range(n_chunks):
        slot = lax.rem(first_slot + chunk, n_slots)
        seq_off, nxt = divmod(chunk + ahead, n_chunks)
        nxt_slot = lax.rem(first_slot + chunk + ahead, n_slots)
        if seq_off == 0:
            start(b, nxt, nxt_slot)
        else:
            @pl.when(b + seq_off < nb)
            def _():
                start(b + seq_off, nxt, nxt_slot)

        wait(b, chunk, slot)
        kc = ckv_buf[slot].reshape(pc * PAGE_SIZE, KV_LORA).astype(BF16)
        s_rope = jnp.concatenate([_dg(q_rope, kr_buf[slot, p].astype(BF16), _NN) for p in range(pc)], axis=1)
        stats.append(block_stats(_dg(q_lat, kc, _NT) + s_rope, kc))

    kn = kn_ref[0]
    s = _dg(q, kn, _NT)
    q_t = lax.broadcasted_iota(jnp.int32, s.shape, 0) // MLA_HEADS
    k_t = lax.broadcasted_iota(jnp.int32, s.shape, 1)
    s = jnp.where((k_t <= q_t) & (k_t < t_new), s, -jnp.inf)
    stats.append(block_stats(s, kn[:, 0:KV_LORA]))
    while len(stats) > 1:
        stats = [merge(stats[i], stats[i + 1]) if i + 1 < len(stats) else stats[i] for i in range(0, len(stats), 2)]
    _, l_sum, acc = stats[0]
    o_ref[0] = (acc / l_sum).astype(o_ref.dtype)


def _paged_attention(page_table, q, k_new, cache_ckv, cache_kr_t, layer, t_new):
    b, rows, _ = q.shape
    n_pages = page_table.shape[1]
    pc = math.gcd(PAGES_PER_CHUNK, n_pages)
    kn_rows = k_new.shape[1]
    grid_spec = pltpu.PrefetchScalarGridSpec(
        num_scalar_prefetch=1,
        grid=(b,),
        in_specs=[pl.BlockSpec((1, rows, QK_DIM), lambda i, pt: (i, 0, 0)),
                  pl.BlockSpec((1, kn_rows, QK_DIM), lambda i, pt: (i, 0, 0)),
                  pl.BlockSpec(memory_space=pl.ANY),
                  pl.BlockSpec(memory_space=pl.ANY)],
        out_specs=pl.BlockSpec((1, rows, KV_LORA), lambda i, pt: (i, 0, 0)),
        scratch_shapes=[pltpu.VMEM((PAGED_SLOTS, pc, PAGE_SIZE, KV_LORA), F32),
                        pltpu.VMEM((PAGED_SLOTS, pc, QK_ROPE, PAGE_SIZE), F32),
                        pltpu.SemaphoreType.DMA((2, PAGED_SLOTS))],
    )
    return pl.pallas_call(
        functools.partial(_paged_kernel, layer=layer, n_pages=n_pages, t_new=t_new),
        out_shape=jax.ShapeDtypeStruct((b, rows, KV_LORA), BF16),
        grid_spec=grid_spec,
        compiler_params=_cparams(("arbitrary",)),
        name="mla_paged",
    )(page_table, q, k_new, cache_ckv, cache_kr_t)


def _merge_kernel(x_ref, y_ref, r_ref, k_ref, v_ref, g_ref, ol_ref, gate_ref, gt_ref, ng_ref,
                  gnw_ref, gnb_ref, rk_ref, avg_ref, ones_ref, wo_ref, wuv_ref, mwo_ref, wout_ref, o_ref):
    y = y_ref[...]
    mean = _dot_exact_rhs(y, avg_ref[...], terms=2)
    yc = y - mean
    var = _dot_exact_rhs(yc * yc, avg_ref[...], terms=2)
    yn = yc * lax.rsqrt(var + GN_EPS) * gnw_ref[...] + gnb_ref[...]
    v = v_ref[...]
    bonus = _dot_exact_rhs(r_ref[...] * k_ref[...] * rk_ref[...], ones_ref[...], terms=2) * v
    o_rw = _dot1((yn + bonus) * g_ref[...], wo_ref[...])
    o_mla = _dot1(_dg(ol_ref[...], wuv_ref[...], _NN), mwo_ref[...])
    gate = gate_ref[...].astype(F32)
    mix = _dot1(gate[:, 0:D_MODEL] * o_rw + gate[:, D_MODEL:] * o_mla, wout_ref[...])
    o_ref[...] = x_ref[...] + gt_ref[0] * _rms(mix, ng_ref[...])


def _merge(x, y, r, k, v, g, o_lat, p_gate, gt, ng, consts, tm, tiles_per_group):
    n = x.shape[0]
    return pl.pallas_call(
        _merge_kernel,
        out_shape=jax.ShapeDtypeStruct((n, D_MODEL), F32),
        grid=(n // tm,),
        in_specs=[_rows(tm, D_MODEL)] + [_rows(tm, RW_DIM)] * 5
                 + [_rows(tm, MLA_HEADS * KV_LORA), _rows(tm, 2 * D_MODEL), _mod_spec(gt, tm, tiles_per_group),
                    _full(ng)] + [_full(c) for c in consts],
        out_specs=_rows(tm, D_MODEL),
        compiler_params=_cparams(("arbitrary",)),
        name="merge",
    )(x, y, r, k, v, g, o_lat, p_gate, gt, ng, *consts)


FFN_CHUNK = 256


def _ffn_kernel(x_ref, g2_ref, sc_ref, sh_ref, gt_ref, g3_ref, win_ref, wout_ref, o_ref, *, d_ff):
    x = x_ref[...]
    hb = (_rms(x, g2_ref[...]) * (1.0 + sc_ref[0]) + sh_ref[0]).astype(BF16)
    acc = jnp.zeros(x.shape, F32)
    for c0 in range(0, d_ff, FFN_CHUNK):
        u = _dg(hb, win_ref[:, c0:c0 + FFN_CHUNK], _NN)
        vv = _dg(hb, win_ref[:, d_ff + c0:d_ff + c0 + FFN_CHUNK], _NN)
        act = (u * _sigmoid(u) * vv).astype(BF16)
        acc = acc + _dg(act, wout_ref[c0:c0 + FFN_CHUNK, :], _NN)
    o_ref[...] = x + gt_ref[0] * _rms(acc, g3_ref[...])


def _ffn(x, g2, sc, sh, gt, g3, w_in, w_out, tm, tiles_per_group):
    n = x.shape[0]
    d_ff = w_out.shape[0]
    ms = lambda m: _mod_spec(m, tm, tiles_per_group)
    return pl.pallas_call(
        functools.partial(_ffn_kernel, d_ff=d_ff),
        out_shape=jax.ShapeDtypeStruct((n, D_MODEL), F32),
        grid=(n // tm,),
        in_specs=[_rows(tm, D_MODEL), _full(g2), ms(sc), ms(sh), ms(gt), _full(g3), _full(w_in), _full(w_out)],
        out_specs=_rows(tm, D_MODEL),
        compiler_params=_cparams(("arbitrary",)),
        name="ffn",
    )(x, g2, sc, sh, gt, g3, w_in, w_out)


def _head_block_matrix(n_heads, width, value):
    idx = np.arange(n_heads * width) // width
    return jnp.asarray(np.where(idx[:, None] == idx[None, :], value, 0.0), dtype=BF16)


def _rotate_half_matrix(n_heads, width):
    half = width // 2
    m = np.zeros((n_heads * width, n_heads * width), np.float32)
    for h in range(n_heads):
        for i in range(half):
            m[h * width + half + i, h * width + i] = -1.0
            m[h * width + i, h * width + half + i] = 1.0
    return jnp.asarray(m, dtype=BF16)


def _rope_tables(pos, reps):
    half = QK_ROPE // 2
    inv = ROPE_THETA ** (-jnp.arange(half, dtype=F32) / half)
    ang = pos.astype(F32)[:, None] * inv[None, :]
    cos = jnp.tile(jnp.cos(ang), (1, 2 * reps))
    sin = jnp.tile(jnp.sin(ang), (1, 2 * reps))
    return cos, sin


def _block_diag_heads(w):
    h, a, b = w.shape
    eye = jnp.eye(h, dtype=w.dtype)
    return (eye[:, None, :, None] * w[:, :, None, :]).reshape(h * a, h * b)


def _state_to_pairs(s):
    b = s.shape[0]
    st = jnp.swapaxes(s, -1, -2).reshape(b, N_PAIRS, 2, RW_HEAD, RW_HEAD)
    eye = jnp.eye(2, dtype=s.dtype)
    return (st[:, :, :, :, None, :] * eye[None, None, :, None, :, None]).reshape(b, N_PAIRS, LANES, LANES)


def _pairs_to_state(h):
    b = h.shape[0]
    h6 = h.reshape(b, N_PAIRS, 2, RW_HEAD, 2, RW_HEAD)
    st = jnp.stack([h6[:, :, 0, :, 0, :], h6[:, :, 1, :, 1, :]], axis=2)
    return jnp.swapaxes(st, -1, -2).reshape(b, RW_HEADS, RW_HEAD, RW_HEAD)


def _prep_weights(P, layer):
    w = {}
    w_in = P['w_in'][layer]
    w['w_rw'] = w_in[:, :RW_COLS].astype(BF16)
    w['w_mla'] = jnp.pad(w_in[:, RW_COLS:RW_COLS + MLA_COLS], ((0, 0), (0, MLA_PAD - MLA_COLS))).astype(BF16)
    w['w_gate'] = w_in[:, RW_COLS + MLA_COLS:].astype(BF16)
    row = lambda z: z.reshape(1, -1)
    w['mu'] = row(P['rw_mu'][layer])
    w['wup'] = jnp.pad(P['rw_w_up'][layer], ((0, LORA_A), (0, 0))).astype(BF16)
    w['aup'] = jnp.pad(P['rw_a_up'][layer], ((LORA_W, 0), (0, 0))).astype(BF16)
    w['gup'] = P['rw_g_up'][layer].astype(BF16)
    for name in ('rw_w0', 'rw_a0', 'rw_k_k', 'rw_k_a', 'rw_gn_w', 'rw_gn_b', 'mla_q_g', 'mla_kv_g'):
        w[name] = row(P[name][layer])
    w['rw_r_k'] = P['rw_r_k'][layer].reshape(1, RW_DIM)
    w['rw_w_o'] = P['rw_w_o'][layer].astype(BF16)
    uq = P['mla_w_uq'][layer].reshape(Q_LORA, MLA_HEADS, QK_NOPE + QK_ROPE)
    w['w_uq'] = jnp.concatenate([uq[:, :, :QK_NOPE].reshape(Q_LORA, -1), uq[:, :, QK_NOPE:].reshape(Q_LORA, -1)],
                                axis=1).astype(BF16)
    w['w_uk'] = _block_diag_heads(jnp.transpose(P['mla_w_uk'][layer], (1, 2, 0))).astype(BF16)
    w['w_uv'] = _block_diag_heads(jnp.transpose(P['mla_w_uv'][layer], (1, 0, 2))).astype(BF16)
    w['mla_w_o'] = P['mla_w_o'][layer].astype(BF16)
    w['w_out'] = P['w_out'][layer].astype(BF16)
    w['ffn_w_in'] = P['ffn_w_in'][layer].astype(BF16)
    w['ffn_w_out'] = P['ffn_w_out'][layer].astype(BF16)
    w['norm_g'] = [row(P['norm_g'][layer][i]) for i in range(4)]
    return w


def _layer(layer, x, c, pos, shift0, wkv0, P, W, consts, attend):
    b, t, _ = x.shape
    n = b * t
    tm = min(256, n)
    xf = x.reshape(n, D_MODEL)
    mod = _mod_matmul(c, P['mod_w'], P['mod_b'], layer)
    if t % tm == 0:
        per_seq = lambda z: z.reshape(b, 1, z.shape[-1])
        tiles_per_group = t // tm
    else:
        per_seq = lambda z: jnp.repeat(z, t, axis=0).reshape(n // tm, tm, z.shape[-1])
        tiles_per_group = 1
    sh1, sc1, gt1, sh2, sc2, gt2 = [per_seq(z) for z in jnp.split(mod, 6, axis=-1)]
    ng = W['norm_g']

    rw_consts = (W['mu'], W['wup'], W['rw_w0'], W['aup'], W['rw_a0'], W['gup'], W['rw_k_k'], W['rw_k_a'],
                 consts['ones_blk'])
    p_rw, p_gate, p_mla, r, lw, k, v, kk, a, g = _in_proj(
        xf, ng[0], sc1, sh1, W['w_rw'], W['w_gate'], W['w_mla'], per_seq(shift0), t, rw_consts, tm, tiles_per_group)

    shift_new = p_rw.reshape(b, t, RW_COLS)[:, -1]
    chunk = 64 if t % 64 == 0 else 8
    t_pad = -(-t // chunk) * chunk
    scan_in = [z.reshape(b, t, RW_DIM) for z in (r, lw, k, v, kk, a)]
    if t_pad != t:
        scan_in = [jnp.pad(z, ((0, 0), (0, t_pad - t), (0, 0))) for z in scan_in]
    y, h_t = _rw_scan(*scan_in, _state_to_pairs(wkv0), chunk, 2 if b % 2 == 0 else 1)
    y = y[:, :t].reshape(n, RW_DIM)
    wkv_new = _pairs_to_state(h_t)

    tma = min(FLASH_TQ, t) if attend is None else tm
    cos, sin = consts['rope'](pos, b, t, tma)
    mla_args = (p_mla, W['mla_q_g'], W['w_uq'], W['w_uk'], W['mla_kv_g'], cos, sin, consts['rot'], tma)
    if attend is None:
        ckv, kr, q_t, k_cat, v_t = _mla_prep(*mla_args, b)
        o_lat = _flash_attention(q_t, k_cat.reshape(b, t, QK_DIM), v_t, tma, min(FLASH_TK, t))
    else:
        ckv, kr, q_lat, q_rope = _mla_prep(*mla_args, None)
        q_cat = jnp.concatenate([q_lat.reshape(n, MLA_HEADS, KV_LORA), q_rope.reshape(n, MLA_HEADS, QK_ROPE)],
                                axis=-1).reshape(b, t * MLA_HEADS, QK_DIM)
        k_cat = jnp.concatenate([ckv, kr], axis=-1).astype(BF16).reshape(b, t, QK_DIM)
        o_lat = attend(layer, q_cat, k_cat).reshape(n, MLA_HEADS * KV_LORA)

    merge_consts = (W['rw_gn_w'], W['rw_gn_b'], W['rw_r_k'], consts['avg_blk'], consts['ones_blk'],
                    W['rw_w_o'], W['w_uv'], W['mla_w_o'], W['w_out'])
    x1 = _merge(xf, y, r, k, v, g, o_lat, p_gate, gt1, ng[1], merge_consts, tm, tiles_per_group)
    x2 = _ffn(x1, ng[2], sc2, sh2, gt2, ng[3], W['ffn_w_in'], W['ffn_w_out'], tm, tiles_per_group)
    return (x2.reshape(b, t, D_MODEL), ckv.reshape(b, t, KV_LORA), kr.reshape(b, t, QK_ROPE), wkv_new, shift_new)


def _trunk(x, c, pos, shift0s, wkv0s, P, Ws, consts, attend):
    depth = shift0s.shape[0]
    ckvs, krs, wkvs, shifts = [], [], [], []
    for layer in range(depth):
        x, ckv, kr, wkv, sh = _layer(layer, x, c, pos, shift0s[layer], wkv0s[layer], P, Ws[layer], consts, attend)
        ckvs.append(ckv)
        krs.append(kr)
        wkvs.append(wkv)
        shifts.append(sh)
    return x, jnp.stack(ckvs), jnp.stack(krs), jnp.stack(wkvs), jnp.stack(shifts)


def kernel(x_prompt, x_sample, cache_ckv, cache_kr, state_wkv, state_shift, page_table, c_prompt, c_sample, mod_w, mod_b, norm_g, w_in, rw_mu, rw_w_up, rw_w0, rw_a_up, rw_a0, rw_g_up, rw_k_k, rw_k_a, rw_r_k, rw_gn_w, rw_gn_b, rw_w_o, mla_q_g, mla_w_uq, mla_kv_g, mla_w_uk, mla_w_uv, mla_w_o, w_out, ffn_w_in, ffn_w_out):
    P = dict(mod_w=mod_w, mod_b=mod_b, norm_g=norm_g, w_in=w_in, rw_mu=rw_mu, rw_w_up=rw_w_up,
             rw_w0=rw_w0, rw_a_up=rw_a_up, rw_a0=rw_a0, rw_g_up=rw_g_up, rw_k_k=rw_k_k,
             rw_k_a=rw_k_a, rw_r_k=rw_r_k, rw_gn_w=rw_gn_w, rw_gn_b=rw_gn_b, rw_w_o=rw_w_o,
             mla_q_g=mla_q_g, mla_w_uq=mla_w_uq, mla_kv_g=mla_kv_g, mla_w_uk=mla_w_uk,
             mla_w_uv=mla_w_uv, mla_w_o=mla_w_o, w_out=w_out, ffn_w_in=ffn_w_in,
             ffn_w_out=ffn_w_out)
    depth = w_in.shape[0]
    Ws = [_prep_weights(P, layer) for layer in range(depth)]

    def rope(pos, b, t, tm):
        cos, sin = _rope_tables(pos, MLA_HEADS)
        if t % tm != 0:
            cos, sin = jnp.tile(cos, (b, 1)), jnp.tile(sin, (b, 1))
        return cos, sin

    consts = dict(ones_blk=_head_block_matrix(RW_HEADS, RW_HEAD, 1.0),
                  avg_blk=_head_block_matrix(RW_HEADS, RW_HEAD, 1.0 / RW_HEAD),
                  rot=_rotate_half_matrix(MLA_HEADS, QK_ROPE),
                  rope=rope)

    bp, tp, _ = x_prompt.shape
    zero_shift = jnp.zeros((depth, bp, RW_COLS), x_prompt.dtype)
    zero_wkv = jnp.zeros((depth, bp, RW_HEADS, RW_HEAD, RW_HEAD), state_wkv.dtype)

    y_prompt, ckv_p, kr_p, wkv_p, shift_p = _trunk(
        x_prompt, c_prompt, jnp.arange(tp), zero_shift, zero_wkv, P, Ws, consts, None)

    bs, ts, _ = x_sample.shape
    past = page_table.shape[1] * PAGE_SIZE
    cache_kr_t = jnp.swapaxes(cache_kr, 2, 3)

    def attend_sample(layer, q_cat, k_cat):
        k_new = jnp.pad(k_cat, ((0, 0), (0, 8 - ts), (0, 0)))
        return _paged_attention(page_table, q_cat, k_new, cache_ckv, cache_kr_t, layer, ts)

    y_sample, ckv_s, kr_s, wkv_s, shift_s = _trunk(
        x_sample, c_sample, past + jnp.arange(ts), state_shift, state_wkv, P, Ws, consts, attend_sample)
    return (y_prompt, y_sample, ckv_p, kr_p, wkv_p, shift_p, ckv_s, kr_s, wkv_s, shift_s)
```

```python
import functools
import math

import jax
import jax.numpy as jnp
import numpy as np
from jax import lax
from jax.experimental import pallas as pl
from jax.experimental.pallas import tpu as pltpu

F32 = jnp.float32
BF16 = jnp.bfloat16

D_MODEL = 1024
RW_HEADS = 8
RW_HEAD = 64
RW_DIM = RW_HEADS * RW_HEAD
LORA_W = 64
LORA_A = 64
LORA_G = 128
RW_COLS = 3 * RW_DIM + LORA_W + LORA_A + LORA_G
MLA_HEADS = 8
QK_NOPE = 64
QK_ROPE = 32
V_HEAD = 64
Q_LORA = 384
KV_LORA = 128
MLA_COLS = Q_LORA + KV_LORA + QK_ROPE
MLA_PAD = 640
QK_DIM = KV_LORA + QK_ROPE
PAGE_SIZE = 128

ROPE_THETA = 10000.0
NORM_EPS = 1e-6
GN_EPS = 64e-5
DECAY_SCALE = math.exp(-0.5)
ATTN_SCALE = (QK_NOPE + QK_ROPE) ** -0.5
LOG2_E = math.log2(math.e)

LANES = 128
N_PAIRS = RW_HEADS // 2
VMEM_LIMIT = 56 * 1024 * 1024


def _cparams(sem):
    return pltpu.CompilerParams(dimension_semantics=sem, vmem_limit_bytes=VMEM_LIMIT)


def _dg(a, b, dims):
    return lax.dot_general(a, b, (dims, ((), ())), preferred_element_type=F32)


_NN = ((1,), (0,))
_NT = ((1,), (1,))


def _dot1(a, b, dims=_NN):
    return _dg(a.astype(BF16), b.astype(BF16), dims)


def _split2(a):
    hi = a.astype(BF16)
    lo = (a - hi.astype(F32)).astype(BF16)
    return hi, lo


def _split3(a):
    hi = a.astype(BF16)
    r = a - hi.astype(F32)
    mid = r.astype(BF16)
    lo = (r - mid.astype(F32)).astype(BF16)
    return hi, mid, lo


def _dot3(a, b, dims=_NN):
    a1, a2 = _split2(a)
    b1, b2 = _split2(b)
    return _dg(a1, b1, dims) + (_dg(a1, b2, dims) + _dg(a2, b1, dims))


def _dot_exact_rhs(a, b_bf16, terms=3):
    if terms == 2:
        a1, a2 = _split2(a)
        return _dg(a1, b_bf16, _NN) + _dg(a2, b_bf16, _NN)
    a1, a2, a3 = _split3(a)
    return _dg(a1, b_bf16, _NN) + (_dg(a2, b_bf16, _NN) + _dg(a3, b_bf16, _NN))


def _dot_exact_lhs(a_bf16, b):
    b1, b2, b3 = _split3(b)
    return _dg(a_bf16, b1, _NN) + (_dg(a_bf16, b2, _NN) + _dg(a_bf16, b3, _NN))


def _rms(x, g):
    return x * lax.rsqrt(jnp.mean(x * x, axis=-1, keepdims=True) + NORM_EPS) * g


def _sigmoid(x):
    return 1.0 / (1.0 + jnp.exp(-x))


def _mod_kernel(c_ref, w_ref, b_ref, o_ref):
    o_ref[...] = _dot3(c_ref[...], w_ref[0]) + b_ref[0]


def _mod_matmul(c, w, b, layer):
    m, k = c.shape
    depth, _, n = w.shape
    tn = 1024
    return pl.pallas_call(
        _mod_kernel,
        out_shape=jax.ShapeDtypeStruct((m, n), F32),
        grid=(n // tn,),
        in_specs=[pl.BlockSpec((m, k), lambda j: (0, 0)),
                  pl.BlockSpec((1, k, tn), lambda j: (layer, 0, j)),
                  pl.BlockSpec((1, 1, tn), lambda j: (layer, 0, j))],
        out_specs=pl.BlockSpec((m, tn), lambda j: (0, j)),
        compiler_params=_cparams(("arbitrary",)),
        name="mod_matmul",
    )(c, w, b.reshape(depth, 1, n))


def _mod_spec(mod, tm, tiles_per_group):
    g, r, d = mod.shape
    if r == 1:
        return pl.BlockSpec((1, 1, d), lambda i: (i // tiles_per_group, 0, 0))
    return pl.BlockSpec((1, r, d), lambda i: (i, 0, 0))


def _full(a):
    nd = a.ndim
    return pl.BlockSpec(a.shape, lambda i: (0,) * nd)


def _rows(tm, width):
    return pl.BlockSpec((tm, width), lambda i: (i, 0))


def _in_proj_kernel(x_ref, ng_ref, sc_ref, sh_ref, wrw_ref, wgt_ref, wml_ref,
                    s0_ref, mu_ref, wup_ref, w0_ref, aup_ref, a0_ref, gup_ref, kk_ref, ka_ref, ones_ref,
                    orw_ref, ogt_ref, oml_ref, r_ref, lw_ref, k_ref, v_ref, kkn_ref, a_ref, g_ref,
                    carry_scr, *, seq_len):
    i = pl.program_id(0)
    h = _rms(x_ref[...], ng_ref[...]) * (1.0 + sc_ref[0]) + sh_ref[0]
    hb = h.astype(BF16)
    p = _dg(hb, wrw_ref[...], _NN)
    orw_ref[...] = p
    ogt_ref[...] = _sigmoid(_dg(hb, wgt_ref[...], _NN)).astype(BF16)
    oml_ref[...] = _dg(hb, wml_ref[...], _NN)
    tm = p.shape[0]

    @pl.when(i == 0)
    def _():
        carry_scr[...] = jnp.zeros(carry_scr.shape, F32)

    row = lax.broadcasted_iota(jnp.int32, (tm, 1), 0)
    above = jnp.where(row == 0, carry_scr[...], pltpu.roll(p, 1, 0))
    prev = jnp.where((i * tm + row) % seq_len == 0, s0_ref[0], above)
    carry_scr[...] = p[tm - 1:tm, :]
    m = p + (prev - p) * mu_ref[...]
    r = m[:, 0:RW_DIM]
    k = m[:, RW_DIM:2 * RW_DIM]
    v = m[:, 2 * RW_DIM:3 * RW_DIM]
    xwa = m[:, 3 * RW_DIM:3 * RW_DIM + LORA_W + LORA_A]
    xg = m[:, 3 * RW_DIM + LORA_W + LORA_A:RW_COLS]
    lw = -DECAY_SCALE * _sigmoid(w0_ref[...] + _dot1(jnp.tanh(xwa), wup_ref[...]))
    a = _sigmoid(a0_ref[...] + _dot1(xwa, aup_ref[...]))
    g = _dot1(_sigmoid(xg), gup_ref[...])
    kk = k * kk_ref[...]
    ss = _dot_exact_rhs(kk * kk, ones_ref[...], terms=2)
    kk = kk / jnp.maximum(jnp.sqrt(ss), 1e-12)
    r_ref[...] = r
    lw_ref[...] = lw
    k_ref[...] = k * (1.0 + (a - 1.0) * ka_ref[...])
    v_ref[...] = v
    kkn_ref[...] = kk
    a_ref[...] = a
    g_ref[...] = g


def _in_proj(x, g, sc, sh, w_rw, w_gate, w_mla, shift0, seq_len, rw_consts, tm, tiles_per_group):
    n = x.shape[0]
    rw_out = jax.ShapeDtypeStruct((n, RW_DIM), F32)
    return pl.pallas_call(
        functools.partial(_in_proj_kernel, seq_len=seq_len),
        out_shape=(jax.ShapeDtypeStruct((n, RW_COLS), F32),
                   jax.ShapeDtypeStruct((n, 2 * D_MODEL), BF16),
                   jax.ShapeDtypeStruct((n, MLA_PAD), F32)) + (rw_out,) * 7,
        grid=(n // tm,),
        in_specs=[_rows(tm, D_MODEL), _full(g), _mod_spec(sc, tm, tiles_per_group),
                  _mod_spec(sh, tm, tiles_per_group), _full(w_rw), _full(w_gate), _full(w_mla),
                  _mod_spec(shift0, tm, tiles_per_group)] + [_full(c) for c in rw_consts],
        out_specs=(_rows(tm, RW_COLS), _rows(tm, 2 * D_MODEL), _rows(tm, MLA_PAD)) + (_rows(tm, RW_DIM),) * 7,
        scratch_shapes=[pltpu.VMEM((1, RW_COLS), F32)],
        compiler_params=_cparams(("arbitrary",)),
        name="in_proj",
    )(x, g, sc, sh, w_rw, w_gate, w_mla, shift0, *rw_consts)


def _rw_scan_kernel(r_ref, lw_ref, k_ref, v_ref, kk_ref, a_ref, h0_ref, y_ref, hT_ref, h_scr, *, chunk, bb):
    c = pl.program_id(1)
    nc = pl.num_programs(1)
    C = chunk

    @pl.when(c == 0)
    def _():
        h_scr[...] = h0_ref[...]

    ti = lax.broadcasted_iota(jnp.int32, (C, C), 0)
    si = lax.broadcasted_iota(jnp.int32, (C, C), 1)
    strict = ti > si
    incl = ti >= si
    ltri = jnp.where(incl, 1.0, 0.0).astype(BF16)
    eye_c = jnp.where(ti == si, 1.0, 0.0).astype(F32)
    lane = lax.broadcasted_iota(jnp.int32, (1, LANES), 1)
    head_mask = [jnp.where(lane < RW_HEAD, 1.0, 0.0).astype(F32), jnp.where(lane >= RW_HEAD, 1.0, 0.0).astype(F32)]
    rj = lax.broadcasted_iota(jnp.int32, (LANES, LANES), 0)
    cj = lax.broadcasted_iota(jnp.int32, (LANES, LANES), 1)
    blockdiag = (rj < RW_HEAD) == (cj < RW_HEAD)
    eye_l = rj == cj
    levels = []
    s = 1
    while s < C:
        levels.append((ti // (2 * s) == si // (2 * s)) & ((ti // s) % 2 == 1) & ((si // s) % 2 == 0))
        s *= 2
    zeros_cl = jnp.zeros((C, LANES), F32)

    pairs = [(bi, p) for bi in range(bb) for p in range(N_PAIRS)]
    heads = [(q, hh) for q in range(len(pairs)) for hh in range(2)]
    load = lambda ref: [ref[bi, :, p * LANES:(p + 1) * LANES] for bi, p in pairs]
    r, lw, k, v, kk, a = (load(ref) for ref in (r_ref, lw_ref, k_ref, v_ref, kk_ref, a_ref))
    H = [h_scr[bi, p] for bi, p in pairs]

    cum = [_dot_exact_lhs(ltri, x) for x in lw]
    cum_last = [x[C - 1:C, :] for x in cum]
    inv_g = [jnp.exp(-x) for x in cum]
    at = [-kk[q] * jnp.exp(cum[q] - lw[q]) for q in range(len(pairs))]
    rt = [r[q] * jnp.exp(cum[q]) for q in range(len(pairs))]
    beta = [kk[q] * a[q] for q in range(len(pairs))]
    bh = [beta[q] * inv_g[q] for q in range(len(pairs))]
    kh = [k[q] * inv_g[q] for q in range(len(pairs))]
    to_end = [jnp.exp(cum_last[q] - cum[q]) for q in range(len(pairs))]
    x_ar = [jnp.concatenate([at[q], rt[q]], axis=0) for q in range(len(pairs))]

    g_b = [_dot1(x_ar[q], bh[q] * head_mask[hh], _NT) for q, hh in heads]
    g_k = [_dot1(x_ar[q], kh[q] * head_mask[hh], _NT) for q, hh in heads]
    a_ab = [jnp.where(strict, g[:C], 0.0) for g in g_b]
    a_rb = [jnp.where(incl, g[C:], 0.0) for g in g_b]
    a_kk = [jnp.concatenate([jnp.where(strict, g[:C], 0.0), jnp.where(incl, g[C:], 0.0)], axis=0) for g in g_k]
    x12 = [_dot1(a_kk[i], v[q]) for i, (q, hh) in enumerate(heads)]
    T = [eye_c + jnp.where(levels[0], x, 0.0) for x in a_ab]
    for lm in levels[1:]:
        ta = [_dot1(T[i], jnp.where(lm, a_ab[i], 0.0)) for i in range(len(heads))]
        T = [T[i] + _dot1(ta[i], T[i]) for i in range(len(heads))]
    tu = [_dot1(T[i], jnp.concatenate([at[q], x12[i][:C]], axis=1)) for i, (q, hh) in enumerate(heads)]
    ry = [_dot1(a_rb[i], tu[i]) for i in range(len(heads))]

    def combine(f):
        return [head_mask[0] * f(2 * q) + head_mask[1] * f(2 * q + 1) for q in range(len(pairs))]

    at_p = combine(lambda i: tu[i][:, :LANES])
    u0_p = combine(lambda i: tu[i][:, LANES:])
    rp_p = combine(lambda i: ry[i][:, :LANES])
    y0_p = combine(lambda i: ry[i][:, LANES:] + x12[i][C:])

    y = [y0_p[q] + _dot3(rt[q] + rp_p[q], H[q]) for q in range(len(pairs))]
    lhs_t = [jnp.concatenate([beta[q] * to_end[q], k[q] * to_end[q]], axis=0).T for q in range(len(pairs))]
    rhs = [jnp.concatenate([jnp.concatenate([at_p[q], u0_p[q]], axis=1),
                            jnp.concatenate([zeros_cl, v[q]], axis=1)], axis=0) for q in range(len(pairs))]
    pq = [_dot1(lhs_t[q], rhs[q]) for q in range(len(pairs))]
    P = [jnp.where(blockdiag, pq[q][:, :LANES], 0.0)
         + jnp.where(eye_l, jnp.broadcast_to(jnp.exp(cum_last[q]), (LANES, LANES)), 0.0) for q in range(len(pairs))]
    h_new = [_dot3(P[q], H[q]) + jnp.where(blockdiag, pq[q][:, LANES:], 0.0) for q in range(len(pairs))]

    for q, (bi, p) in enumerate(pairs):
        y_ref[bi, :, p * LANES:(p + 1) * LANES] = y[q]
        h_scr[bi, p] = h_new[q]

    @pl.when(c == nc - 1)
    def _():
        hT_ref[...] = h_scr[...]


def _rw_scan(r, lw, k, v, kk, a, h0, chunk, bb):
    b, t, _ = r.shape
    nc = t // chunk
    row = pl.BlockSpec((bb, chunk, RW_DIM), lambda g, c: (g, c, 0))
    hspec = pl.BlockSpec((bb, N_PAIRS, LANES, LANES), lambda g, c: (g, 0, 0, 0))
    return pl.pallas_call(
        functools.partial(_rw_scan_kernel, chunk=chunk, bb=bb),
        out_shape=(jax.ShapeDtypeStruct((b, t, RW_DIM), F32),
                   jax.ShapeDtypeStruct((b, N_PAIRS, LANES, LANES), F32)),
        grid=(b // bb, nc),
        in_specs=[row] * 6 + [hspec],
        out_specs=(row, hspec),
        scratch_shapes=[pltpu.VMEM((bb, N_PAIRS, LANES, LANES), F32)],
        compiler_params=_cparams(("arbitrary", "arbitrary")),
        name="rw_scan",
    )(r, lw, k, v, kk, a, h0)


def _mla_prep_kernel(p_ref, qg_ref, wuq_ref, wuk_ref, kvg_ref, cos_ref, sin_ref, rot_ref, *out_refs, flash_layout):
    p = p_ref[...]
    tm = p.shape[0]
    cq = p[:, 0:Q_LORA]
    ckv = p[:, Q_LORA:Q_LORA + KV_LORA]
    kr = p[:, Q_LORA + KV_LORA:MLA_COLS]
    cos = cos_ref[...]
    sin = sin_ref[...]
    rot = rot_ref[...]
    q = _dot1(_rms(cq, qg_ref[...]), wuq_ref[...])
    q_nope = q[:, 0:MLA_HEADS * QK_NOPE]
    q_rope = q[:, MLA_HEADS * QK_NOPE:]
    q_rope = q_rope * cos + _dot_exact_rhs(q_rope, rot, terms=2) * sin
    q_scale = ATTN_SCALE * LOG2_E
    q_lat = _dot1(q_nope, wuk_ref[...]) * q_scale
    q_rope = q_rope * q_scale
    ckv_n = _rms(ckv, kvg_ref[...])
    kr_r = kr * cos[:, 0:QK_ROPE] + _dot_exact_rhs(kr, rot[0:QK_ROPE, 0:QK_ROPE]) * sin[:, 0:QK_ROPE]
    out_refs[0][...] = ckv_n
    out_refs[1][...] = kr_r
    if flash_layout:
        _, _, qt_ref, kc_ref, vt_ref = out_refs
        q_lat_t = q_lat.T
        q_rope_t = q_rope.T
        for h in range(MLA_HEADS):
            cols = slice(h * tm, (h + 1) * tm)
            qt_ref[0, 0:KV_LORA, cols] = q_lat_t[h * KV_LORA:(h + 1) * KV_LORA].astype(BF16)
            qt_ref[0, KV_LORA:QK_DIM, cols] = q_rope_t[h * QK_ROPE:(h + 1) * QK_ROPE].astype(BF16)
        kc_ref[...] = jnp.concatenate([ckv_n, kr_r], axis=1).astype(BF16)
        vt_ref[0, 0:KV_LORA, :] = ckv_n.T.astype(BF16)
        vt_ref[0, KV_LORA:VT_ROWS, :] = jnp.ones((VT_ROWS - KV_LORA, tm), BF16)
    else:
        _, _, ql_ref, qr_ref = out_refs
        ql_ref[...] = q_lat.astype(BF16)
        qr_ref[...] = q_rope.astype(BF16)


def _mla_prep(p_mla, q_g, w_uq, w_uk_bd, kv_g, cos, sin, rot, tm, flash_batch):
    n = p_mla.shape[0]
    nt = n // tm
    tab_tiles = cos.shape[0] // tm
    tab = pl.BlockSpec((tm, MLA_HEADS * QK_ROPE), lambda i: (i % tab_tiles, 0))
    out_shape = [jax.ShapeDtypeStruct((n, KV_LORA), F32), jax.ShapeDtypeStruct((n, QK_ROPE), F32)]
    out_specs = [_rows(tm, KV_LORA), _rows(tm, QK_ROPE)]
    if flash_batch is not None:
        tpb = nt // flash_batch
        out_shape += [jax.ShapeDtypeStruct((nt, QK_DIM, MLA_HEADS * tm), BF16),
                      jax.ShapeDtypeStruct((n, QK_DIM), BF16),
                      jax.ShapeDtypeStruct((flash_batch, VT_ROWS, n // flash_batch), BF16)]
        out_specs += [pl.BlockSpec((1, QK_DIM, MLA_HEADS * tm), lambda i: (i, 0, 0)),
                      _rows(tm, QK_DIM),
                      pl.BlockSpec((1, VT_ROWS, tm), lambda i: (i // tpb, 0, i % tpb))]
    else:
        out_shape += [jax.ShapeDtypeStruct((n, MLA_HEADS * KV_LORA), BF16),
                      jax.ShapeDtypeStruct((n, MLA_HEADS * QK_ROPE), BF16)]
        out_specs += [_rows(tm, MLA_HEADS * KV_LORA), _rows(tm, MLA_HEADS * QK_ROPE)]
    return pl.pallas_call(
        functools.partial(_mla_prep_kernel, flash_layout=flash_batch is not None),
        out_shape=tuple(out_shape),
        grid=(nt,),
        in_specs=[_rows(tm, MLA_PAD), _full(q_g), _full(w_uq), _full(w_uk_bd), _full(kv_g), tab, tab, _full(rot)],
        out_specs=tuple(out_specs),
        compiler_params=_cparams(("arbitrary",)),
        name="mla_prep",
    )(p_mla, q_g, w_uq, w_uk_bd, kv_g, cos, sin, rot)


FLASH_TQ = 512
FLASH_TK = 512
VT_ROWS = KV_LORA + 16


def _flash_kernel(qi_ref, kj_ref, qT_ref, k_ref, vT_ref, o_ref, m_scr, acc_scr, *, tq, tk):
    step = pl.program_id(1)
    i = qi_ref[step]
    j = kj_ref[step]
    rows = tq * MLA_HEADS
    q_lo = i * tq
    k_lo = j * tk
    last_j = (q_lo + tq - 1) // tk

    @pl.when(j == 0)
    def _():
        m_scr[...] = jnp.full((1, rows), -jnp.inf, F32)
        acc_scr[...] = jnp.zeros((VT_ROWS, rows), F32)

    def update(masked):
        s = _dg(k_ref[0], qT_ref[0], _NN)
        if masked:
            k_pos = k_lo + lax.broadcasted_iota(jnp.int32, s.shape, 0)
            q_pos = q_lo + lax.broadcasted_iota(jnp.int32, s.shape, 1) % tq
            s = jnp.where(k_pos <= q_pos, s, -jnp.inf)
        m_old = m_scr[...]
        m_new = jnp.maximum(m_old, jnp.max(s, axis=0, keepdims=True))
        pe = jnp.exp2(s - m_new).astype(BF16)
        acc_scr[...] = acc_scr[...] * jnp.exp2(m_old - m_new) + _dg(vT_ref[0], pe, _NN)
        m_scr[...] = m_new

    @pl.when(j < last_j)
    def _():
        update(False)

    @pl.when(j == last_j)
    def _():
        update(True)
        o = acc_scr[0:KV_LORA, :] / acc_scr[KV_LORA:KV_LORA + 1, :]
        for h in range(MLA_HEADS):
            o_ref[:, h * KV_LORA:(h + 1) * KV_LORA] = o[:, h * tq:(h + 1) * tq].T.astype(o_ref.dtype)


def _flash_attention(q_t, k, v_t, tq, tk):
    b, t, _ = k.shape
    assert tq <= tk and tk % tq == 0, "only the last key block of a query block may straddle the diagonal"
    rows = tq * MLA_HEADS
    nq = t // tq
    qi = np.concatenate([np.full(((i * tq + tq - 1) // tk + 1,), i, np.int32) for i in range(nq)])
    kj = np.concatenate([np.arange((i * tq + tq - 1) // tk + 1, dtype=np.int32) for i in range(nq)])
    grid_spec = pltpu.PrefetchScalarGridSpec(
        num_scalar_prefetch=2,
        grid=(b, len(qi)),
        in_specs=[pl.BlockSpec((1, QK_DIM, rows), lambda bb, s, qi, kj: (bb * nq + qi[s], 0, 0)),
                  pl.BlockSpec((1, tk, QK_DIM), lambda bb, s, qi, kj: (bb, kj[s], 0)),
                  pl.BlockSpec((1, VT_ROWS, tk), lambda bb, s, qi, kj: (bb, 0, kj[s]))],
        out_specs=pl.BlockSpec((tq, MLA_HEADS * KV_LORA), lambda bb, s, qi, kj: (bb * nq + qi[s], 0)),
        scratch_shapes=[pltpu.VMEM((1, rows), F32), pltpu.VMEM((VT_ROWS, rows), F32)],
    )
    return pl.pallas_call(
        functools.partial(_flash_kernel, tq=tq, tk=tk),
        out_shape=jax.ShapeDtypeStruct((b * t, MLA_HEADS * KV_LORA), BF16),
        grid_spec=grid_spec,
        compiler_params=_cparams(("arbitrary", "arbitrary")),
        name="mla_flash",
    )(jnp.asarray(qi), jnp.asarray(kj), q_t, k, v_t)


PAGES_PER_CHUNK = 64
PAGED_SLOTS = 3


def _paged_kernel(pt_ref, q_ref, kn_ref, ckv_hbm, krt_hbm, o_ref, ckv_buf, kr_buf, sem, *, layer, n_pages, t_new):
    b = pl.program_id(0)
    nb = pl.num_programs(0)
    pc = ckv_buf.shape[1]
    n_chunks = n_pages // pc
    rows = q_ref.shape[1]

    def copies(seq, chunk, slot, p):
        page = pt_ref[seq, chunk * pc + p]
        return (pltpu.make_async_copy(ckv_hbm.at[layer, page], ckv_buf.at[slot, p], sem.at[0, slot]),
                pltpu.make_async_copy(krt_hbm.at[layer, page], kr_buf.at[slot, p], sem.at[1, slot]))

    def start(seq, chunk, slot):
        for p in range(pc):
            for cp in copies(seq, chunk, slot, p):
                cp.start()

    def wait(seq, chunk, slot):
        for p in range(pc):
            for cp in copies(seq, chunk, slot, p):
                cp.wait()

    q = q_ref[0]
    q_lat = q[:, 0:KV_LORA]
    q_rope = q[:, KV_LORA:QK_DIM]

    def block_stats(s, vals):
        m = jnp.max(s, axis=-1, keepdims=True)
        pe = jnp.exp2(s - m)
        return m, jnp.sum(pe, axis=-1, keepdims=True), _dg(pe.astype(BF16), vals, _NN)

    def merge(x, y):
        m = jnp.maximum(x[0], y[0])
        cx = jnp.exp2(x[0] - m)
        cy = jnp.exp2(y[0] - m)
        return m, x[1] * cx + y[1] * cy, x[2] * cx + y[2] * cy

    n_slots = ckv_buf.shape[0]
    ahead = n_slots - 1
    first_slot = lax.rem(b * n_chunks, n_slots)

    @pl.when(b == 0)
    def _():
        for g in range(min(ahead, pt_ref.shape[0] * n_chunks)):
            start(g // n_chunks, g % n_chunks, g % n_slots)

    stats = []
    for chunk in range(n_chunks):
        slot = lax.rem(first_slot + chunk, n_slots)
        seq_off, nxt = divmod(chunk + ahead, n_chunks)
        nxt_slot = lax.rem(first_slot + chunk + ahead, n_slots)
        if seq_off == 0:
            start(b, nxt, nxt_slot)
        else:
            @pl.when(b + seq_off < nb)
            def _():
                start(b + seq_off, nxt, nxt_slot)

        wait(b, chunk, slot)
        kc = ckv_buf[slot].reshape(pc * PAGE_SIZE, KV_LORA).astype(BF16)
        s_rope = jnp.concatenate([_dg(q_rope, kr_buf[slot, p].astype(BF16), _NN) for p in range(pc)], axis=1)
        stats.append(block_stats(_dg(q_lat, kc, _NT) + s_rope, kc))

    kn = kn_ref[0]
    s = _dg(q, kn, _NT)
    q_t = lax.broadcasted_iota(jnp.int32, s.shape, 0) // MLA_HEADS
    k_t = lax.broadcasted_iota(jnp.int32, s.shape, 1)
    s = jnp.where((k_t <= q_t) & (k_t < t_new), s, -jnp.inf)
    stats.append(block_stats(s, kn[:, 0:KV_LORA]))
    while len(stats) > 1:
        stats = [merge(stats[i], stats[i + 1]) if i + 1 < len(stats) else stats[i] for i in range(0, len(stats), 2)]
    _, l_sum, acc = stats[0]
    o_ref[0] = (acc / l_sum).astype(o_ref.dtype)


def _paged_attention(page_table, q, k_new, cache_ckv, cache_kr_t, layer, t_new):
    b, rows, _ = q.shape
    n_pages = page_table.shape[1]
    pc = math.gcd(PAGES_PER_CHUNK, n_pages)
    kn_rows = k_new.shape[1]
    grid_spec = pltpu.PrefetchScalarGridSpec(
        num_scalar_prefetch=1,
        grid=(b,),
        in_specs=[pl.BlockSpec((1, rows, QK_DIM), lambda i, pt: (i, 0, 0)),
                  pl.BlockSpec((1, kn_rows, QK_DIM), lambda i, pt: (i, 0, 0)),
                  pl.BlockSpec(memory_space=pl.ANY),
                  pl.BlockSpec(memory_space=pl.ANY)],
        out_specs=pl.BlockSpec((1, rows, KV_LORA), lambda i, pt: (i, 0, 0)),
        scratch_shapes=[pltpu.VMEM((PAGED_SLOTS, pc, PAGE_SIZE, KV_LORA), F32),
                        pltpu.VMEM((PAGED_SLOTS, pc, QK_ROPE, PAGE_SIZE), F32),
                        pltpu.SemaphoreType.DMA((2, PAGED_SLOTS))],
    )
    return pl.pallas_call(
        functools.partial(_paged_kernel, layer=layer, n_pages=n_pages, t_new=t_new),
        out_shape=jax.ShapeDtypeStruct((b, rows, KV_LORA), BF16),
        grid_spec=grid_spec,
        compiler_params=_cparams(("arbitrary",)),
        name="mla_paged",
    )(page_table, q, k_new, cache_ckv, cache_kr_t)


def _merge_kernel(x_ref, y_ref, r_ref, k_ref, v_ref, g_ref, ol_ref, gate_ref, gt_ref, ng_ref,
                  gnw_ref, gnb_ref, rk_ref, avg_ref, ones_ref, wo_ref, wuv_ref, mwo_ref, wout_ref, o_ref):
    y = y_ref[...]
    mean = _dot_exact_rhs(y, avg_ref[...], terms=2)
    yc = y - mean
    var = _dot_exact_rhs(yc * yc, avg_ref[...], terms=2)
    yn = yc * lax.rsqrt(var + GN_EPS) * gnw_ref[...] + gnb_ref[...]
    v = v_ref[...]
    bonus = _dot_exact_rhs(r_ref[...] * k_ref[...] * rk_ref[...], ones_ref[...], terms=2) * v
    o_rw = _dot1((yn + bonus) * g_ref[...], wo_ref[...])
    o_mla = _dot1(_dg(ol_ref[...], wuv_ref[...], _NN), mwo_ref[...])
    gate = gate_ref[...].astype(F32)
    mix = _dot1(gate[:, 0:D_MODEL] * o_rw + gate[:, D_MODEL:] * o_mla, wout_ref[...])
    o_ref[...] = x_ref[...] + gt_ref[0] * _rms(mix, ng_ref[...])


def _merge(x, y, r, k, v, g, o_lat, p_gate, gt, ng, consts, tm, tiles_per_group):
    n = x.shape[0]
    return pl.pallas_call(
        _merge_kernel,
        out_shape=jax.ShapeDtypeStruct((n, D_MODEL), F32),
        grid=(n // tm,),
        in_specs=[_rows(tm, D_MODEL)] + [_rows(tm, RW_DIM)] * 5
                 + [_rows(tm, MLA_HEADS * KV_LORA), _rows(tm, 2 * D_MODEL), _mod_spec(gt, tm, tiles_per_group),
                    _full(ng)] + [_full(c) for c in consts],
        out_specs=_rows(tm, D_MODEL),
        compiler_params=_cparams(("arbitrary",)),
        name="merge",
    )(x, y, r, k, v, g, o_lat, p_gate, gt, ng, *consts)


FFN_CHUNK = 256


def _ffn_kernel(x_ref, g2_ref, sc_ref, sh_ref, gt_ref, g3_ref, win_ref, wout_ref, o_ref, *, d_ff):
    x = x_ref[...]
    hb = (_rms(x, g2_ref[...]) * (1.0 + sc_ref[0]) + sh_ref[0]).astype(BF16)
    acc = jnp.zeros(x.shape, F32)
    for c0 in range(0, d_ff, FFN_CHUNK):
        u = _dg(hb, win_ref[:, c0:c0 + FFN_CHUNK], _NN)
        vv = _dg(hb, win_ref[:, d_ff + c0:d_ff + c0 + FFN_CHUNK], _NN)
        act = (u * _sigmoid(u) * vv).astype(BF16)
        acc = acc + _dg(act, wout_ref[c0:c0 + FFN_CHUNK, :], _NN)
    o_ref[...] = x + gt_ref[0] * _rms(acc, g3_ref[...])


def _ffn(x, g2, sc, sh, gt, g3, w_in, w_out, tm, tiles_per_group):
    n = x.shape[0]
    d_ff = w_out.shape[0]
    ms = lambda m: _mod_spec(m, tm, tiles_per_group)
    return pl.pallas_call(
        functools.partial(_ffn_kernel, d_ff=d_ff),
        out_shape=jax.ShapeDtypeStruct((n, D_MODEL), F32),
        grid=(n // tm,),
        in_specs=[_rows(tm, D_MODEL), _full(g2), ms(sc), ms(sh), ms(gt), _full(g3), _full(w_in), _full(w_out)],
        out_specs=_rows(tm, D_MODEL),
        compiler_params=_cparams(("arbitrary",)),
        name="ffn",
    )(x, g2, sc, sh, gt, g3, w_in, w_out)


def _head_block_matrix(n_heads, width, value):
    idx = np.arange(n_heads * width) // width
    return jnp.asarray(np.where(idx[:, None] == idx[None, :], value, 0.0), dtype=BF16)


def _rotate_half_matrix(n_heads, width):
    half = width // 2
    m = np.zeros((n_heads * width, n_heads * width), np.float32)
    for h in range(n_heads):
        for i in range(half):
            m[h * width + half + i, h * width + i] = -1.0
            m[h * width + i, h * width + half + i] = 1.0
    return jnp.asarray(m, dtype=BF16)


def _rope_tables(pos, reps):
    half = QK_ROPE // 2
    inv = ROPE_THETA ** (-jnp.arange(half, dtype=F32) / half)
    ang = pos.astype(F32)[:, None] * inv[None, :]
    cos = jnp.tile(jnp.cos(ang), (1, 2 * reps))
    sin = jnp.tile(jnp.sin(ang), (1, 2 * reps))
    return cos, sin


def _block_diag_heads(w):
    h, a, b = w.shape
    eye = jnp.eye(h, dtype=w.dtype)
    return (eye[:, None, :, None] * w[:, :, None, :]).reshape(h * a, h * b)


def _state_to_pairs(s):
    b = s.shape[0]
    st = jnp.swapaxes(s, -1, -2).reshape(b, N_PAIRS, 2, RW_HEAD, RW_HEAD)
    eye = jnp.eye(2, dtype=s.dtype)
    return (st[:, :, :, :, None, :] * eye[None, None, :, None, :, None]).reshape(b, N_PAIRS, LANES, LANES)


def _pairs_to_state(h):
    b = h.shape[0]
    h6 = h.reshape(b, N_PAIRS, 2, RW_HEAD, 2, RW_HEAD)
    st = jnp.stack([h6[:, :, 0, :, 0, :], h6[:, :, 1, :, 1, :]], axis=2)
    return jnp.swapaxes(st, -1, -2).reshape(b, RW_HEADS, RW_HEAD, RW_HEAD)


def _prep_weights(P, layer):
    w = {}
    w_in = P['w_in'][layer]
    w['w_rw'] = w_in[:, :RW_COLS].astype(BF16)
    w['w_mla'] = jnp.pad(w_in[:, RW_COLS:RW_COLS + MLA_COLS], ((0, 0), (0, MLA_PAD - MLA_COLS))).astype(BF16)
    w['w_gate'] = w_in[:, RW_COLS + MLA_COLS:].astype(BF16)
    row = lambda z: z.reshape(1, -1)
    w['mu'] = row(P['rw_mu'][layer])
    w['wup'] = jnp.pad(P['rw_w_up'][layer], ((0, LORA_A), (0, 0))).astype(BF16)
    w['aup'] = jnp.pad(P['rw_a_up'][layer], ((LORA_W, 0), (0, 0))).astype(BF16)
    w['gup'] = P['rw_g_up'][layer].astype(BF16)
    for name in ('rw_w0', 'rw_a0', 'rw_k_k', 'rw_k_a', 'rw_gn_w', 'rw_gn_b', 'mla_q_g', 'mla_kv_g'):
        w[name] = row(P[name][layer])
    w['rw_r_k'] = P['rw_r_k'][layer].reshape(1, RW_DIM)
    w['rw_w_o'] = P['rw_w_o'][layer].astype(BF16)
    uq = P['mla_w_uq'][layer].reshape(Q_LORA, MLA_HEADS, QK_NOPE + QK_ROPE)
    w['w_uq'] = jnp.concatenate([uq[:, :, :QK_NOPE].reshape(Q_LORA, -1), uq[:, :, QK_NOPE:].reshape(Q_LORA, -1)],
                                axis=1).astype(BF16)
    w['w_uk'] = _block_diag_heads(jnp.transpose(P['mla_w_uk'][layer], (1, 2, 0))).astype(BF16)
    w['w_uv'] = _block_diag_heads(jnp.transpose(P['mla_w_uv'][layer], (1, 0, 2))).astype(BF16)
    w['mla_w_o'] = P['mla_w_o'][layer].astype(BF16)
    w['w_out'] = P['w_out'][layer].astype(BF16)
    w['ffn_w_in'] = P['ffn_w_in'][layer].astype(BF16)
    w['ffn_w_out'] = P['ffn_w_out'][layer].astype(BF16)
    w['norm_g'] = [row(P['norm_g'][layer][i]) for i in range(4)]
    return w


def _layer(layer, x, c, pos, shift0, wkv0, P, W, consts, attend):
    b, t, _ = x.shape
    n = b * t
    tm = min(256, n)
    xf = x.reshape(n, D_MODEL)
    mod = _mod_matmul(c, P['mod_w'], P['mod_b'], layer)
    if t % tm == 0:
        per_seq = lambda z: z.reshape(b, 1, z.shape[-1])
        tiles_per_group = t // tm
    else:
        per_seq = lambda z: jnp.repeat(z, t, axis=0).reshape(n // tm, tm, z.shape[-1])
        tiles_per_group = 1
    sh1, sc1, gt1, sh2, sc2, gt2 = [per_seq(z) for z in jnp.split(mod, 6, axis=-1)]
    ng = W['norm_g']

    rw_consts = (W['mu'], W['wup'], W['rw_w0'], W['aup'], W['rw_a0'], W['gup'], W['rw_k_k'], W['rw_k_a'],
                 consts['ones_blk'])
    p_rw, p_gate, p_mla, r, lw, k, v, kk, a, g = _in_proj(
        xf, ng[0], sc1, sh1, W['w_rw'], W['w_gate'], W['w_mla'], per_seq(shift0), t, rw_consts, tm, tiles_per_group)

    shift_new = p_rw.reshape(b, t, RW_COLS)[:, -1]
    chunk = 64 if t % 64 == 0 else 8
    t_pad = -(-t // chunk) * chunk
    scan_in = [z.reshape(b, t, RW_DIM) for z in (r, lw, k, v, kk, a)]
    if t_pad != t:
        scan_in = [jnp.pad(z, ((0, 0), (0, t_pad - t), (0, 0))) for z in scan_in]
    seqs_per_step = 4 if (b % 4 == 0 and chunk < 64) else (2 if b % 2 == 0 else 1)
    y, h_t = _rw_scan(*scan_in, _state_to_pairs(wkv0), chunk, seqs_per_step)
    y = y[:, :t].reshape(n, RW_DIM)
    wkv_new = _pairs_to_state(h_t)

    tma = min(FLASH_TQ, t) if attend is None else tm
    cos, sin = consts['rope'](pos, b, t, tma)
    mla_args = (p_mla, W['mla_q_g'], W['w_uq'], W['w_uk'], W['mla_kv_g'], cos, sin, consts['rot'], tma)
    if attend is None:
        ckv, kr, q_t, k_cat, v_t = _mla_prep(*mla_args, b)
        o_lat = _flash_attention(q_t, k_cat.reshape(b, t, QK_DIM), v_t, tma, min(FLASH_TK, t))
    else:
        ckv, kr, q_lat, q_rope = _mla_prep(*mla_args, None)
        q_cat = jnp.concatenate([q_lat.reshape(n, MLA_HEADS, KV_LORA), q_rope.reshape(n, MLA_HEADS, QK_ROPE)],
                                axis=-1).reshape(b, t * MLA_HEADS, QK_DIM)
        k_cat = jnp.concatenate([ckv, kr], axis=-1).astype(BF16).reshape(b, t, QK_DIM)
        o_lat = attend(layer, q_cat, k_cat).reshape(n, MLA_HEADS * KV_LORA)

    merge_consts = (W['rw_gn_w'], W['rw_gn_b'], W['rw_r_k'], consts['avg_blk'], consts['ones_blk'],
                    W['rw_w_o'], W['w_uv'], W['mla_w_o'], W['w_out'])
    x1 = _merge(xf, y, r, k, v, g, o_lat, p_gate, gt1, ng[1], merge_consts, tm, tiles_per_group)
    x2 = _ffn(x1, ng[2], sc2, sh2, gt2, ng[3], W['ffn_w_in'], W['ffn_w_out'], tm, tiles_per_group)
    return (x2.reshape(b, t, D_MODEL), ckv.reshape(b, t, KV_LORA), kr.reshape(b, t, QK_ROPE), wkv_new, shift_new)


def _trunk(x, c, pos, shift0s, wkv0s, P, Ws, consts, attend):
    depth = shift0s.shape[0]
    ckvs, krs, wkvs, shifts = [], [], [], []
    for layer in range(depth):
        x, ckv, kr, wkv, sh = _layer(layer, x, c, pos, shift0s[layer], wkv0s[layer], P, Ws[layer], consts, attend)
        ckvs.append(ckv)
        krs.append(kr)
        wkvs.append(wkv)
        shifts.append(sh)
    return x, jnp.stack(ckvs), jnp.stack(krs), jnp.stack(wkvs), jnp.stack(shifts)


def kernel(x_prompt, x_sample, cache_ckv, cache_kr, state_wkv, state_shift, page_table, c_prompt, c_sample, mod_w, mod_b, norm_g, w_in, rw_mu, rw_w_up, rw_w0, rw_a_up, rw_a0, rw_g_up, rw_k_k, rw_k_a, rw_r_k, rw_gn_w, rw_gn_b, rw_w_o, mla_q_g, mla_w_uq, mla_kv_g, mla_w_uk, mla_w_uv, mla_w_o, w_out, ffn_w_in, ffn_w_out):
    P = dict(mod_w=mod_w, mod_b=mod_b, norm_g=norm_g, w_in=w_in, rw_mu=rw_mu, rw_w_up=rw_w_up,
             rw_w0=rw_w0, rw_a_up=rw_a_up, rw_a0=rw_a0, rw_g_up=rw_g_up, rw_k_k=rw_k_k,
             rw_k_a=rw_k_a, rw_r_k=rw_r_k, rw_gn_w=rw_gn_w, rw_gn_b=rw_gn_b, rw_w_o=rw_w_o,
             mla_q_g=mla_q_g, mla_w_uq=mla_w_uq, mla_kv_g=mla_kv_g, mla_w_uk=mla_w_uk,
             mla_w_uv=mla_w_uv, mla_w_o=mla_w_o, w_out=w_out, ffn_w_in=ffn_w_in,
             ffn_w_out=ffn_w_out)
    depth = w_in.shape[0]
    Ws = [_prep_weights(P, layer) for layer in range(depth)]

    def rope(pos, b, t, tm):
        cos, sin = _rope_tables(pos, MLA_HEADS)
        if t % tm != 0:
            cos, sin = jnp.tile(cos, (b, 1)), jnp.tile(sin, (b, 1))
        return cos, sin

    consts = dict(ones_blk=_head_block_matrix(RW_HEADS, RW_HEAD, 1.0),
                  avg_blk=_head_block_matrix(RW_HEADS, RW_HEAD, 1.0 / RW_HEAD),
                  rot=_rotate_half_matrix(MLA_HEADS, QK_ROPE),
                  rope=rope)

    bp, tp, _ = x_prompt.shape
    zero_shift = jnp.zeros((depth, bp, RW_COLS), x_prompt.dtype)
    zero_wkv = jnp.zeros((depth, bp, RW_HEADS, RW_HEAD, RW_HEAD), state_wkv.dtype)

    y_prompt, ckv_p, kr_p, wkv_p, shift_p = _trunk(
        x_prompt, c_prompt, jnp.arange(tp), zero_shift, zero_wkv, P, Ws, consts, None)

    bs, ts, _ = x_sample.shape
    past = page_table.shape[1] * PAGE_SIZE
    cache_kr_t = jnp.swapaxes(cache_kr, 2, 3)

    def attend_sample(layer, q_cat, k_cat):
        k_new = jnp.pad(k_cat, ((0, 0), (0, 8 - ts), (0, 0)))
        return _paged_attention(page_table, q_cat, k_new, cache_ckv, cache_kr_t, layer, ts)

    y_sample, ckv_s, kr_s, wkv_s, shift_s = _trunk(
        x_sample, c_sample, past + jnp.arange(ts), state_shift, state_wkv, P, Ws, consts, attend_sample)
    return (y_prompt, y_sample, ckv_p, kr_p, wkv_p, shift_p, ckv_s, kr_s, wkv_s, shift_s)
```
